```python
import math
import jax, jax.numpy as jnp
from jax import lax
import numpy as np

D_MODEL = 2048
BATCH = 1
SEQ = 16384
DEPTH = 2

HEAD_DIM_A = 128
HEADS_PER_GROUP_A = 4
DILATED_GROUPS = ((128, 1), (512, 4), (2048, 16))
N_GROUPS_A = len(DILATED_GROUPS)
N_HEADS_A = N_GROUPS_A * HEADS_PER_GROUP_A
WIDTH_A = N_HEADS_A * HEAD_DIM_A
OUT_WIDTH_A = HEADS_PER_GROUP_A * HEAD_DIM_A
N_HEADS_B = 16
QK_NOPE = 128
QK_ROPE = 64
V_DIM = 128
Q_LORA = 512
KV_LORA = 512
QK_DIM_B = QK_NOPE + QK_ROPE
OUT_WIDTH_B = N_HEADS_B * V_DIM
N_BRANCH = 2
N_IN = 3 * WIDTH_A + Q_LORA + KV_LORA + QK_ROPE + N_BRANCH * D_MODEL
D_FF = 5632
CONV_WIDTH = 3
ROPE_THETA = 10000.0
EPS = 1e-6
Q_BLOCK = 128

kernel_name = "hybrid_dilated_mla_convffn_encoder"


def rms_norm(x, g):
    xf = x.astype(jnp.float32)
    y = xf * lax.rsqrt(jnp.mean(xf * xf, axis=-1, keepdims=True) + EPS)
    return (y * g.astype(jnp.float32)).astype(x.dtype)


def rope(x, positions):
    dim = x.shape[-1]
    inv_freq = 1.0 / (ROPE_THETA ** (jnp.arange(0, dim, 2, dtype=jnp.float32) / dim))
    ang = positions.astype(jnp.float32)[..., None] * inv_freq
    cos = jnp.cos(ang)[:, :, None, :]
    sin = jnp.sin(ang)[:, :, None, :]
    xf = x.astype(jnp.float32)
    x1, x2 = xf[..., : dim // 2], xf[..., dim // 2:]
    out = jnp.concatenate([x1 * cos - x2 * sin, x2 * cos + x1 * sin], axis=-1)
    return out.astype(x.dtype)


def dilated_window_attention(q, k, v, window, dilation):
    B, S, H, hd = q.shape
    d = dilation
    w = window // (2 * d)
    L = S // d
    nb = -(-L // w)
    Lp = nb * w
    G = B * d

    def to_sub(t):
        return t.reshape(B, L, d, H, hd).transpose(0, 2, 1, 3, 4).reshape(G, L, H, hd)

    def windows(t):
        tp = jnp.pad(t, ((0, 0), (w, Lp - L + w), (0, 0), (0, 0))).reshape(G, nb + 2, w, H, hd)
        return jnp.concatenate([tp[:, :-2], tp[:, 1:-1], tp[:, 2:]], axis=2)

    qb = jnp.pad(to_sub(q), ((0, 0), (0, Lp - L), (0, 0), (0, 0))).reshape(G, nb, w, H, hd)
    kw = windows(to_sub(k))
    vw = windows(to_sub(v))

    start = jnp.arange(nb)[:, None] * w
    qi = start + jnp.arange(w)[None, :]
    kj = start - w + jnp.arange(3 * w)[None, :]
    dist = kj[:, None, :] - qi[:, :, None]
    valid = (jnp.abs(dist) <= w) & (kj[:, None, :] >= 0) & (kj[:, None, :] < L)

    s = jnp.einsum('gnqhd,gnkhd->ghnqk', qb, kw, preferred_element_type=jnp.float32) * (hd ** -0.5)
    s = jnp.where(valid, s, -jnp.inf)
    m = jnp.max(s, axis=-1, keepdims=True)
    p = jnp.exp(s - m)
    den = jnp.sum(p, axis=-1, keepdims=True)
    o = jnp.einsum('ghnqk,gnkhd->gnqhd', p.astype(v.dtype), vw, preferred_element_type=jnp.float32)
    o = o / den.transpose(0, 2, 3, 1, 4)
    lse = (m + jnp.log(den))[..., 0].transpose(0, 2, 3, 1)

    o = o.reshape(G, Lp, H, hd)[:, :L].reshape(B, d, L, H, hd).transpose(0, 2, 1, 3, 4).reshape(B, S, H, hd)
    lse = lse.reshape(G, Lp, H)[:, :L].reshape(B, d, L, H).transpose(0, 2, 1, 3).reshape(B, S, H)
    return o, lse


def dense_attention(q, k, v, scale):
    B, S, H, dq = q.shape
    dv = v.shape[-1]
    nq = S // Q_BLOCK
    qb = q.reshape(B, nq, Q_BLOCK, H, dq).transpose(1, 0, 2, 3, 4)

    def block(qi):
        s = jnp.einsum('bqhd,bkhd->bhqk', qi, k, preferred_element_type=jnp.float32) * scale
        p = jax.nn.softmax(s, axis=-1)
        o = jnp.einsum('bhqk,bkhd->bqhd', p.astype(v.dtype), v, preferred_element_type=jnp.float32)
        return o.astype(v.dtype)

    out = lax.map(block, qb)
    return out.transpose(1, 0, 2, 3, 4).reshape(B, S, H, dv)


def centred_depthwise_conv(h, w, b):
    S = h.shape[1]
    pad = CONV_WIDTH // 2
    hp = jnp.pad(h, ((0, 0), (pad, CONV_WIDTH - 1 - pad), (0, 0)))
    out = b
    for t in range(CONV_WIDTH):
        out = out + hp[:, t:t + S] * w[t]
    return out


def setup_inputs(seed: int = 0) -> dict:
    key = jax.random.key(seed)
    ks = jax.random.split(key, 20)

    def nrm(k, shape, scale):
        return jax.random.normal(k, shape, jnp.float32) * scale

    x = nrm(ks[0], (BATCH, SEQ, D_MODEL), 1.0)
    offset = jax.random.randint(ks[1], (BATCH, 1), 0, 1024, dtype=jnp.int32)
    positions = (offset + jnp.arange(SEQ, dtype=jnp.int32)[None, :]).astype(jnp.int32)
    return {
        "x": x,
        "positions": positions,
        "norm_mix": 1.0 + nrm(ks[2], (DEPTH, D_MODEL), 0.02),
        "w_in": nrm(ks[3], (DEPTH, D_MODEL, N_IN), D_MODEL ** -0.5),
        "b_gate": nrm(ks[4], (DEPTH, N_BRANCH, D_MODEL), 0.02),
        "norm_q": 1.0 + nrm(ks[5], (DEPTH, Q_LORA), 0.02),
        "w_uq": nrm(ks[6], (DEPTH, Q_LORA, N_HEADS_B * QK_DIM_B), Q_LORA ** -0.5),
        "norm_kv": 1.0 + nrm(ks[7], (DEPTH, KV_LORA), 0.02),
        "w_ukv": nrm(ks[8], (DEPTH, KV_LORA, N_HEADS_B * (QK_NOPE + V_DIM)), KV_LORA ** -0.5),
        "w_oa": nrm(ks[9], (DEPTH, OUT_WIDTH_A, D_MODEL), OUT_WIDTH_A ** -0.5),
        "w_ob": nrm(ks[10], (DEPTH, OUT_WIDTH_B, D_MODEL), OUT_WIDTH_B ** -0.5),
        "w_out": nrm(ks[11], (DEPTH, D_MODEL, D_MODEL), D_MODEL ** -0.5),
        "norm_ffn": 1.0 + nrm(ks[12], (DEPTH, D_MODEL), 0.02),
        "w_up": nrm(ks[13], (DEPTH, D_MODEL, 2 * D_FF), D_MODEL ** -0.5),
        "conv_w": nrm(ks[14], (DEPTH, CONV_WIDTH, 2 * D_FF), CONV_WIDTH ** -0.5),
        "conv_b": nrm(ks[15], (DEPTH, 2 * D_FF), 0.02),
        "w_down": nrm(ks[16], (DEPTH, D_FF, D_MODEL), D_FF ** -0.5),
        "norm_final": 1.0 + nrm(ks[17], (D_MODEL,), 0.02),
    }


def reference(x, positions, norm_mix, w_in, b_gate, norm_q, w_uq, norm_kv, w_ukv,
              w_oa, w_ob, w_out, norm_ffn, w_up, conv_w, conv_b, w_down, norm_final):
    B, S, D = x.shape
    o_qa, o_ka, o_va = 0, WIDTH_A, 2 * WIDTH_A
    o_ql = 3 * WIDTH_A
    o_kvl = o_ql + Q_LORA
    o_kr = o_kvl + KV_LORA
    o_g = o_kr + QK_ROPE

    for l in range(DEPTH):
        h = rms_norm(x, norm_mix[l])
        p = h @ w_in[l]

        qa = rope(p[..., o_qa:o_qa + WIDTH_A].reshape(B, S, N_HEADS_A, HEAD_DIM_A), positions)
        ka = rope(p[..., o_ka:o_ka + WIDTH_A].reshape(B, S, N_HEADS_A, HEAD_DIM_A), positions)
        va = p[..., o_va:o_va + WIDTH_A].reshape(B, S, N_HEADS_A, HEAD_DIM_A)
        outs, lses = [], []
        for gi, (window, dilation) in enumerate(DILATED_GROUPS):
            hs = slice(gi * HEADS_PER_GROUP_A, (gi + 1) * HEADS_PER_GROUP_A)
            o_g_i, lse_g_i = dilated_window_attention(qa[:, :, hs], ka[:, :, hs], va[:, :, hs], window, dilation)
            outs.append(o_g_i)
            lses.append(lse_g_i)
        wgt = jax.nn.softmax(jnp.stack(lses, axis=0), axis=0)
        oa = jnp.sum(wgt[..., None] * jnp.stack(outs, axis=0), axis=0).astype(x.dtype)
        ya = oa.reshape(B, S, OUT_WIDTH_A) @ w_oa[l]

        cq = rms_norm(p[..., o_ql:o_ql + Q_LORA], norm_q[l]) @ w_uq[l]
        cq = cq.reshape(B, S, N_HEADS_B, QK_DIM_B)
        q_b = jnp.concatenate([cq[..., :QK_NOPE], rope(cq[..., QK_NOPE:], positions)], axis=-1)
        ckv = rms_norm(p[..., o_kvl:o_kvl + KV_LORA], norm_kv[l]) @ w_ukv[l]
        ckv = ckv.reshape(B, S, N_HEADS_B, QK_NOPE + V_DIM)
        k_pe = rope(p[..., o_kr:o_kr + QK_ROPE][:, :, None, :], positions)
        k_b = jnp.concatenate([ckv[..., :QK_NOPE],
                               jnp.broadcast_to(k_pe, (B, S, N_HEADS_B, QK_ROPE))], axis=-1)
        v_b = ckv[..., QK_NOPE:]
        ob = dense_attention(q_b, k_b, v_b, QK_DIM_B ** -0.5)
        yb = ob.reshape(B, S, OUT_WIDTH_B) @ w_ob[l]

        gates = jax.nn.sigmoid(p[..., o_g:o_g + N_BRANCH * D].reshape(B, S, N_BRANCH, D) + b_gate[l])
        merged = gates[:, :, 0] * ya + gates[:, :, 1] * yb
        x = x + merged @ w_out[l]

        h2 = rms_norm(x, norm_ffn[l])
        u = centred_depthwise_conv(h2 @ w_up[l], conv_w[l], conv_b[l])
        x = x + (jax.nn.silu(u[..., :D_FF]) * u[..., D_FF:]) @ w_down[l]

    return rms_norm(x, norm_final)
```

```python
import functools
import math

import jax
import jax.numpy as jnp
from jax import lax
from jax.experimental import pallas as pl
from jax.experimental.pallas import tpu as pltpu

F32 = jnp.float32
BF16 = jnp.bfloat16

HEAD_DIM_A = 128
HEADS_PER_GROUP_A = 4
DILATED_GROUPS = ((128, 1), (512, 4), (2048, 16))
GROUP_WIDTH_A = HEADS_PER_GROUP_A * HEAD_DIM_A
WIDTH_A = len(DILATED_GROUPS) * GROUP_WIDTH_A
N_HEADS_B = 16
QK_NOPE = 128
QK_ROPE = 64
V_DIM = 128
Q_LORA = 512
KV_LORA = 512
QK_DIM_B = QK_NOPE + QK_ROPE
ROPE_THETA = 10000.0
EPS = 1e-6
CONV_WIDTH = 3

LANES = 128
MASK_VALUE = -1e30
VMEM_LIMIT = 48 * 1024 * 1024


def _params(*sem):
    return pltpu.CompilerParams(dimension_semantics=sem, vmem_limit_bytes=VMEM_LIMIT)


def _rmsnorm_body(x_ref, g_ref, o_ref):
    x = x_ref[...].astype(F32)
    ms = jnp.mean(x * x, axis=-1, keepdims=True)
    o_ref[...] = (x * lax.rsqrt(ms + EPS) * g_ref[...]).astype(o_ref.dtype)


def rmsnorm(x, g, out_dtype, *, col_block=0, width=None, tm=512):
    s = x.shape[0]
    width = x.shape[1] if width is None else width
    tm = min(tm, s)
    return pl.pallas_call(
        _rmsnorm_body,
        grid=(s // tm,),
        in_specs=[
            pl.BlockSpec((tm, width), lambda i: (i, col_block)),
            pl.BlockSpec((1, width), lambda i: (0, 0)),
        ],
        out_specs=pl.BlockSpec((tm, width), lambda i: (i, 0)),
        out_shape=jax.ShapeDtypeStruct((s, width), out_dtype),
        compiler_params=_params("parallel"),
        name="rmsnorm",
    )(x, g.reshape(1, width).astype(F32))


def _mm_body(a_ref, b_ref, *rest, epilogue, n_extra):
    acc = jnp.dot(a_ref[...], b_ref[...], preferred_element_type=F32)
    epilogue(acc, rest[:n_extra], rest[n_extra:])


def matmul(a, b, epilogue, extra, extra_specs, out_shapes, out_specs, *, tm, tn, name):
    m, k = a.shape
    n = b.shape[1]
    tm = min(tm, m)
    grid = (m // tm, n // tn)
    return pl.pallas_call(
        functools.partial(_mm_body, epilogue=epilogue, n_extra=len(extra)),
        grid=grid,
        in_specs=[
            pl.BlockSpec((tm, k), lambda i, j: (i, 0)),
            pl.BlockSpec((k, tn), lambda i, j: (0, j)),
            *extra_specs,
        ],
        out_specs=out_specs,
        out_shape=out_shapes,
        compiler_params=_params("parallel", "arbitrary"),
        name=name,
    )(a, b, *extra)


def _rope128_epilogue(acc, extra, outs):
    cos = extra[0][0]
    sin = extra[1][0]
    (o_ref,) = outs
    for c in range(acc.shape[1] // LANES):
        x = acc[:, c * LANES:(c + 1) * LANES]
        y = x * cos + pltpu.roll(x, LANES // 2, 1) * sin
        o_ref[:, c * LANES:(c + 1) * LANES] = y.astype(o_ref.dtype)


def _store_epilogue(acc, extra, outs):
    outs[0][...] = acc.astype(outs[0].dtype)


def _rope64_epilogue(acc, extra, outs):
    cos = extra[0][...]
    sin = extra[1][...]
    lane = lax.broadcasted_iota(jnp.int32, acc.shape, 1)
    half = QK_ROPE // 2
    rot = jnp.where(lane < half, pltpu.roll(acc, LANES - half, 1), pltpu.roll(acc, half, 1))
    outs[0][...] = (acc * cos + rot * sin).astype(outs[0].dtype)


def _sigmoid_epilogue(acc, extra, outs):
    z = acc + extra[0][...]
    outs[0][...] = (1.0 / (1.0 + jnp.exp(-z))).astype(outs[0].dtype)


def _residual_epilogue(acc, extra, outs):
    outs[0][...] = extra[0][...] + acc


def _uq_body(a_ref, w_ref, cos_ref, sin_ref, o_ref, *, scale):
    acc = jnp.dot(a_ref[...], w_ref[...], preferred_element_type=F32)
    half = QK_ROPE // 2
    r = acc[:, QK_NOPE:]
    rot = jnp.concatenate([r[:, half:], r[:, :half]], axis=1)
    r = r * cos_ref[...] + rot * sin_ref[...]
    o_ref[:, :QK_NOPE] = (acc[:, :QK_NOPE] * scale).astype(o_ref.dtype)
    o_ref[:, QK_NOPE:] = (r * scale).astype(o_ref.dtype)


def mla_q_proj(cqn, w_uq_h, cos64, sin64, *, tm=1024):
    s = cqn.shape[0]
    tm = min(tm, s)
    return pl.pallas_call(
        functools.partial(_uq_body, scale=QK_DIM_B ** -0.5),
        grid=(s // tm, N_HEADS_B),
        in_specs=[
            pl.BlockSpec((tm, Q_LORA), lambda i, h: (i, 0)),
            pl.BlockSpec((None, Q_LORA, QK_DIM_B), lambda i, h: (h, 0, 0)),
            pl.BlockSpec((tm, QK_ROPE), lambda i, h: (i, 0)),
            pl.BlockSpec((tm, QK_ROPE), lambda i, h: (i, 0)),
        ],
        out_specs=pl.BlockSpec((None, tm, QK_DIM_B), lambda i, h: (h, i, 0)),
        out_shape=jax.ShapeDtypeStruct((N_HEADS_B, s, QK_DIM_B), BF16),
        compiler_params=_params("parallel", "arbitrary"),
        name="mla_q_proj",
    )(cqn, w_uq_h, cos64, sin64)


def _ukv_body(a_ref, w_ref, kr_ref, k_ref, v_ref):
    acc = jnp.dot(a_ref[...], w_ref[...], preferred_element_type=F32)
    k_ref[:, :QK_NOPE] = acc[:, :QK_NOPE].astype(k_ref.dtype)
    k_ref[:, QK_NOPE:] = kr_ref[:, :QK_ROPE].astype(k_ref.dtype)
    v_ref[...] = acc[:, QK_NOPE:].astype(v_ref.dtype)


def mla_kv_proj(ckvn, w_ukv, kr, *, tm=1024):
    s = ckvn.shape[0]
    tm = min(tm, s)
    hw = QK_NOPE + V_DIM
    return pl.pallas_call(
        _ukv_body,
        grid=(s // tm, N_HEADS_B),
        in_specs=[
            pl.BlockSpec((tm, KV_LORA), lambda i, h: (i, 0)),
            pl.BlockSpec((KV_LORA, hw), lambda i, h: (0, h)),
            pl.BlockSpec((tm, LANES), lambda i, h: (i, 0)),
        ],
        out_specs=[
            pl.BlockSpec((None, tm, QK_DIM_B), lambda i, h: (h, i, 0)),
            pl.BlockSpec((None, tm, V_DIM), lambda i, h: (h, i, 0)),
        ],
        out_shape=[
            jax.ShapeDtypeStruct((N_HEADS_B, s, QK_DIM_B), BF16),
            jax.ShapeDtypeStruct((N_HEADS_B, s, V_DIM), BF16),
        ],
        compiler_params=_params("parallel", "arbitrary"),
        name="mla_kv_proj",
    )(ckvn, w_ukv, kr)


def _flash_body(q_ref, k_ref, v_ref, o_ref, *, tk):
    q = q_ref[...]
    tq = q.shape[0]
    n_kv = k_ref.shape[0] // tk

    def step(j, carry):
        m, l, acc = carry
        off = pl.multiple_of(j * tk, tk)
        k = k_ref[pl.ds(off, tk), :]
        v = v_ref[pl.ds(off, tk), :]
        s = lax.dot_general(q, k, (((1,), (1,)), ((), ())), preferred_element_type=F32)
        m_new = jnp.maximum(m, jnp.max(s, axis=1, keepdims=True))
        alpha = jnp.exp(m - m_new)
        p = jnp.exp(s - m_new)
        l = alpha * l + jnp.sum(p, axis=1, keepdims=True)
        acc = alpha * acc + jnp.dot(p.astype(v.dtype), v, preferred_element_type=F32)
        return m_new, l, acc

    m0 = jnp.full((tq, 1), MASK_VALUE, F32)
    l0 = jnp.zeros((tq, 1), F32)
    a0 = jnp.zeros((tq, v_ref.shape[1]), F32)
    _, l, acc = lax.fori_loop(0, n_kv, step, (m0, l0, a0))
    o_ref[...] = (acc / l).astype(o_ref.dtype)


def dense_attention(q, k, v, *, tq=256, tk=512):
    h, s, dq = q.shape
    dv = v.shape[2]
    tq = min(tq, s)
    tk = min(tk, s)
    return pl.pallas_call(
        functools.partial(_flash_body, tk=tk),
        grid=(h, s // tq),
        in_specs=[
            pl.BlockSpec((None, tq, dq), lambda hh, i: (hh, i, 0)),
            pl.BlockSpec((None, s, dq), lambda hh, i: (hh, 0, 0)),
            pl.BlockSpec((None, s, dv), lambda hh, i: (hh, 0, 0)),
        ],
        out_specs=pl.BlockSpec((tq, dv), lambda hh, i: (i, hh)),
        out_shape=jax.ShapeDtypeStruct((s, h * dv), BF16),
        compiler_params=_params("parallel", "arbitrary"),
        name="mla_flash",
    )(q, k, v)


def _dilated_steps(t):
    steps = []
    for g, (window, d) in enumerate(DILATED_GROUPS):
        n = -(-(window // 2) // t)
        steps += [(g, o) for o in range(-n, n + 1)]
    return steps


def _step_lookup(step, values):
    out = jnp.int32(values[-1])
    for idx in range(len(values) - 2, -1, -1):
        out = jnp.where(step <= idx, jnp.int32(values[idx]), out)
    return out


def _dilated_body(q_ref, k_ref, v_ref, o_ref, m_sc, l_sc, acc_sc, *, t, groups, offsets, nblk):
    i = pl.program_id(0)
    step = pl.program_id(1)
    nsteps = pl.num_programs(1)

    @pl.when(step == 0)
    def _():
        m_sc[...] = jnp.full(m_sc.shape, MASK_VALUE, F32)
        l_sc[...] = jnp.zeros(l_sc.shape, F32)
        acc_sc[...] = jnp.zeros(acc_sc.shape, F32)

    off = _step_lookup(step, offsets)
    g = _step_lookup(step, groups)
    dil = _step_lookup(step, [DILATED_GROUPS[gg][1] for gg in groups])
    half = _step_lookup(step, [DILATED_GROUPS[gg][0] // 2 for gg in groups])
    kb = i + off

    @pl.when(jnp.logical_and(kb >= 0, kb < nblk))
    def _():
        row = lax.broadcasted_iota(jnp.int32, (t, t), 0)
        col = lax.broadcasted_iota(jnp.int32, (t, t), 1)
        dist = col - row + off * t
        valid = (jnp.abs(dist) <= half) & ((dist & (dil - 1)) == 0)
        for hh in range(HEADS_PER_GROUP_A):
            sl = slice(hh * HEAD_DIM_A, (hh + 1) * HEAD_DIM_A)
            q = q_ref[:, sl]
            k = k_ref[:, sl]
            v = v_ref[:, sl]
            s = lax.dot_general(q, k, (((1,), (1,)), ((), ())), preferred_element_type=F32)
            s = jnp.where(valid, s, MASK_VALUE)
            m_prev = m_sc[hh]
            m_new = jnp.maximum(m_prev, jnp.max(s, axis=1, keepdims=True))
            alpha = jnp.exp(m_prev - m_new)
            p = jnp.where(valid, jnp.exp(s - m_new), 0.0)
            l_sc[hh] = alpha * l_sc[hh] + jnp.sum(p, axis=1, keepdims=True)
            acc_sc[:, sl] = alpha * acc_sc[:, sl] + jnp.dot(p.astype(v.dtype), v, preferred_element_type=F32)
            m_sc[hh] = m_new

    @pl.when(step == nsteps - 1)
    def _():
        for hh in range(HEADS_PER_GROUP_A):
            sl = slice(hh * HEAD_DIM_A, (hh + 1) * HEAD_DIM_A)
            o_ref[:, sl] = (acc_sc[:, sl] / l_sc[hh]).astype(o_ref.dtype)


def dilated_attention(qk, v, *, t=256):
    s = qk.shape[0]
    t = min(t, s)
    nblk = s // t
    steps = _dilated_steps(t)
    groups = [g for g, _ in steps]
    offsets = [o for _, o in steps]
    n_g = len(DILATED_GROUPS)

    def q_map(i, st):
        return (i, _step_lookup(st, groups))

    def k_map(i, st):
        return (jnp.clip(i + _step_lookup(st, offsets), 0, nblk - 1), n_g + _step_lookup(st, groups))

    def v_map(i, st):
        return (jnp.clip(i + _step_lookup(st, offsets), 0, nblk - 1), _step_lookup(st, groups))

    return pl.pallas_call(
        functools.partial(_dilated_body, t=t, groups=groups, offsets=offsets, nblk=nblk),
        grid=(nblk, len(steps)),
        in_specs=[
            pl.BlockSpec((t, GROUP_WIDTH_A), q_map),
            pl.BlockSpec((t, GROUP_WIDTH_A), k_map),
            pl.BlockSpec((t, GROUP_WIDTH_A), v_map),
        ],
        out_specs=pl.BlockSpec((t, GROUP_WIDTH_A), lambda i, st: (i, 0)),
        out_shape=jax.ShapeDtypeStruct((s, GROUP_WIDTH_A), BF16),
        scratch_shapes=[
            pltpu.VMEM((HEADS_PER_GROUP_A, t, 1), F32),
            pltpu.VMEM((HEADS_PER_GROUP_A, t, 1), F32),
            pltpu.VMEM((t, GROUP_WIDTH_A), F32),
        ],
        compiler_params=_params("parallel", "arbitrary"),
        name="dilated_attn",
    )(qk, qk, v)


def _merge_body(oa_ref, ob_ref, woa_ref, wob_ref, g0_ref, g1_ref, o_ref):
    ya = jnp.dot(oa_ref[...], woa_ref[...], preferred_element_type=F32)
    yb = jnp.dot(ob_ref[...], wob_ref[...], preferred_element_type=F32)
    o_ref[...] = (g0_ref[...] * ya + g1_ref[...] * yb).astype(o_ref.dtype)


def gated_merge(oa, ob, w_oa, w_ob, gates, *, tm=1024, tn=512):
    s = oa.shape[0]
    d = w_oa.shape[1]
    tm = min(tm, s)
    nj = d // tn
    return pl.pallas_call(
        _merge_body,
        grid=(s // tm, nj),
        in_specs=[
            pl.BlockSpec((tm, oa.shape[1]), lambda i, j: (i, 0)),
            pl.BlockSpec((tm, ob.shape[1]), lambda i, j: (i, 0)),
            pl.BlockSpec((w_oa.shape[0], tn), lambda i, j: (0, j)),
            pl.BlockSpec((w_ob.shape[0], tn), lambda i, j: (0, j)),
            pl.BlockSpec((tm, tn), lambda i, j: (i, j)),
            pl.BlockSpec((tm, tn), lambda i, j: (i, j + nj)),
        ],
        out_specs=pl.BlockSpec((tm, tn), lambda i, j: (i, j)),
        out_shape=jax.ShapeDtypeStruct((s, d), BF16),
        compiler_params=_params("parallel", "arbitrary"),
        name="gated_merge",
    )(oa, ob, w_oa, w_ob, gates, gates)


HALO = 16


def _ffn_body(x_ref, xp_ref, xn_ref, g_ref, wa_ref, wb_ref, cwa_ref, cwb_ref, cba_ref, cbb_ref, wd_ref,
              o_ref, h_sc):
    i = pl.program_id(0)
    j = pl.program_id(1)
    tm = x_ref.shape[0]

    def norm(x):
        ms = jnp.mean(x * x, axis=-1, keepdims=True)
        return x * lax.rsqrt(ms + EPS) * g_ref[...]

    @pl.when(j == 0)
    def _():
        x = x_ref[...]
        o_ref[...] = x
        h_sc[HALO:HALO + tm, :] = norm(x).astype(h_sc.dtype)
        hp = jnp.where(i > 0, norm(xp_ref[...]), 0.0)
        hn = jnp.where(i < pl.num_programs(0) - 1, norm(xn_ref[...]), 0.0)
        h_sc[:HALO, :] = hp.astype(h_sc.dtype)
        h_sc[HALO + tm:, :] = hn.astype(h_sc.dtype)

    h = h_sc[...]
    rows = tm + 2 * HALO

    def conv(w_ref, cw_ref, cb_ref):
        up = jnp.dot(h, w_ref[...], preferred_element_type=F32)
        cw = cw_ref[...]
        prev = pltpu.roll(up, 1, 0)
        nxt = pltpu.roll(up, rows - 1, 0)
        u = cb_ref[...] + prev * cw[0:1] + up * cw[1:2] + nxt * cw[2:3]
        return u[HALO:HALO + tm]

    ua = conv(wa_ref, cwa_ref, cba_ref)
    ub = conv(wb_ref, cwb_ref, cbb_ref)
    act = (ua / (1.0 + jnp.exp(-ua))) * ub
    o_ref[...] += jnp.dot(act.astype(wd_ref.dtype), wd_ref[...], preferred_element_type=F32)


def conv_ffn(x, g, w_up, conv_w, conv_b, w_down, *, tm=512, tf=512):
    s, d = x.shape
    f = w_down.shape[0]
    tm = min(tm, s)
    nf = f // tf
    hb = tm // HALO
    n_halo_blocks = s // HALO
    return pl.pallas_call(
        _ffn_body,
        grid=(s // tm, nf),
        in_specs=[
            pl.BlockSpec((tm, d), lambda i, j: (i, 0)),
            pl.BlockSpec((HALO, d), lambda i, j: (jnp.maximum(i * hb - 1, 0), 0)),
            pl.BlockSpec((HALO, d), lambda i, j: (jnp.minimum((i + 1) * hb, n_halo_blocks - 1), 0)),
            pl.BlockSpec((1, d), lambda i, j: (0, 0)),
            pl.BlockSpec((d, tf), lambda i, j: (0, j)),
            pl.BlockSpec((d, tf), lambda i, j: (0, j + nf)),
            pl.BlockSpec((CONV_WIDTH, tf), lambda i, j: (0, j)),
            pl.BlockSpec((CONV_WIDTH, tf), lambda i, j: (0, j + nf)),
            pl.BlockSpec((1, tf), lambda i, j: (0, j)),
            pl.BlockSpec((1, tf), lambda i, j: (0, j + nf)),
            pl.BlockSpec((tf, d), lambda i, j: (j, 0)),
        ],
        out_specs=pl.BlockSpec((tm, d), lambda i, j: (i, 0)),
        out_shape=jax.ShapeDtypeStruct((s, d), F32),
        scratch_shapes=[pltpu.VMEM((tm + 2 * HALO, d), BF16)],
        compiler_params=_params("parallel", "arbitrary"),
        name="conv_ffn",
    )(x, x, x, g.reshape(1, d), w_up, w_up, conv_w, conv_w, conv_b.reshape(1, -1), conv_b.reshape(1, -1), w_down)


def _rope_tables(positions, dim):
    inv_freq = 1.0 / (ROPE_THETA ** (jnp.arange(0, dim, 2, dtype=F32) / dim))
    ang = positions.astype(F32)[:, None] * inv_freq
    cos = jnp.cos(ang)
    sin = jnp.sin(ang)
    return jnp.concatenate([cos, cos], axis=1), jnp.concatenate([-sin, sin], axis=1)


def kernel(x, positions, norm_mix, w_in, b_gate, norm_q, w_uq, norm_kv, w_ukv, w_oa, w_ob, w_out,
           norm_ffn, w_up, conv_w, conv_b, w_down, norm_final):
    batch, s, d = x.shape
    assert batch == 1
    depth = w_in.shape[0]
    xs = x.reshape(s, d)
    pos = positions.reshape(s)

    cos_a, sin_a = _rope_tables(pos, HEAD_DIM_A)
    scale_a = HEAD_DIM_A ** -0.5
    cos_qk = jnp.stack([cos_a * scale_a, cos_a])
    sin_qk = jnp.stack([sin_a * scale_a, sin_a])
    cos_b, sin_b = _rope_tables(pos, QK_ROPE)
    pad = ((0, 0), (0, LANES - QK_ROPE))
    cos_bp, sin_bp = jnp.pad(cos_b, pad), jnp.pad(sin_b, pad)

    o_ql = 3 * WIDTH_A
    o_kvl = o_ql + Q_LORA
    o_kr = o_kvl + KV_LORA
    o_g = o_kr + QK_ROPE
    tn = 512
    n_q_tiles = WIDTH_A // tn

    for l in range(depth):
        w_in_l = w_in[l].astype(BF16)
        w_qk = w_in_l[:, :2 * WIDTH_A]
        w_v = w_in_l[:, 2 * WIDTH_A:o_ql]
        w_lat = w_in_l[:, o_ql:o_kr]
        w_kr = jnp.pad(w_in_l[:, o_kr:o_g], ((0, 0), (0, LANES - QK_ROPE)))
        w_g = w_in_l[:, o_g:]

        h = rmsnorm(xs, norm_mix[l], BF16)
        tm = min(1024, s)

        qk = matmul(
            h, w_qk, _rope128_epilogue, (cos_qk, sin_qk),
            [pl.BlockSpec((1, tm, LANES), lambda i, j: (jnp.where(j < n_q_tiles, 0, 1), i, 0))] * 2,
            jax.ShapeDtypeStruct((s, 2 * WIDTH_A), BF16), pl.BlockSpec((tm, tn), lambda i, j: (i, j)),
            tm=tm, tn=tn, name="proj_qk")
        v = matmul(
            h, w_v, _store_epilogue, (), [],
            jax.ShapeDtypeStruct((s, WIDTH_A), BF16), pl.BlockSpec((tm, tn), lambda i, j: (i, j)),
            tm=tm, tn=tn, name="proj_v")
        lat = matmul(
            h, w_lat, _store_epilogue, (), [],
            jax.ShapeDtypeStruct((s, Q_LORA + KV_LORA), F32), pl.BlockSpec((tm, tn), lambda i, j: (i, j)),
            tm=tm, tn=tn, name="proj_lat")
        kr = matmul(
            h, w_kr, _rope64_epilogue, (cos_bp, sin_bp),
            [pl.BlockSpec((tm, LANES), lambda i, j: (i, 0))] * 2,
            jax.ShapeDtypeStruct((s, LANES), BF16), pl.BlockSpec((tm, LANES), lambda i, j: (i, j)),
            tm=tm, tn=LANES, name="proj_kr")
        gates = matmul(
            h, w_g, _sigmoid_epilogue, (b_gate[l].reshape(1, -1),),
            [pl.BlockSpec((1, tn), lambda i, j: (0, j))],
            jax.ShapeDtypeStruct((s, 2 * d), F32), pl.BlockSpec((tm, tn), lambda i, j: (i, j)),
            tm=tm, tn=tn, name="proj_gates")

        oa = dilated_attention(qk, v)

        cqn = rmsnorm(lat, norm_q[l], BF16, col_block=0, width=Q_LORA)
        ckvn = rmsnorm(lat, norm_kv[l], BF16, col_block=1, width=KV_LORA)
        w_uq_h = w_uq[l].astype(BF16).reshape(Q_LORA, N_HEADS_B, QK_DIM_B).transpose(1, 0, 2)
        q_b = mla_q_proj(cqn, w_uq_h, cos_b, sin_b)
        k_b, v_b = mla_kv_proj(ckvn, w_ukv[l].astype(BF16), kr)
        ob = dense_attention(q_b, k_b, v_b)

        merged = gated_merge(oa, ob, w_oa[l].astype(BF16), w_ob[l].astype(BF16), gates)
        xs = matmul(
            merged, w_out[l].astype(BF16), _residual_epilogue, (xs,),
            [pl.BlockSpec((tm, tn), lambda i, j: (i, j))],
            jax.ShapeDtypeStruct((s, d), F32), pl.BlockSpec((tm, tn), lambda i, j: (i, j)),
            tm=tm, tn=tn, name="proj_out")

        xs = conv_ffn(xs, norm_ffn[l], w_up[l].astype(BF16), conv_w[l], conv_b[l], w_down[l].astype(BF16))

    out = rmsnorm(xs, norm_final, F32)
    return out.reshape(batch, s, d)
```

```python
import functools
import math

import jax
import jax.numpy as jnp
from jax import lax
from jax.experimental import pallas as pl
from jax.experimental.pallas import tpu as pltpu

F32 = jnp.float32
BF16 = jnp.bfloat16

HEAD_DIM_A = 128
HEADS_PER_GROUP_A = 4
DILATED_GROUPS = ((128, 1), (512, 4), (2048, 16))
GROUP_WIDTH_A = HEADS_PER_GROUP_A * HEAD_DIM_A
WIDTH_A = len(DILATED_GROUPS) * GROUP_WIDTH_A
N_HEADS_B = 16
QK_NOPE = 128
QK_ROPE = 64
V_DIM = 128
Q_LORA = 512
KV_LORA = 512
QK_DIM_B = QK_NOPE + QK_ROPE
ROPE_THETA = 10000.0
EPS = 1e-6
CONV_WIDTH = 3

LANES = 128
MASK_VALUE = -1e30
VMEM_LIMIT = 48 * 1024 * 1024


def _params(*sem):
    return pltpu.CompilerParams(dimension_semantics=sem, vmem_limit_bytes=VMEM_LIMIT)


def _rmsnorm_body(x_ref, g_ref, o_ref):
    x = x_ref[...].astype(F32)
    ms = jnp.mean(x * x, axis=-1, keepdims=True)
    o_ref[...] = (x * lax.rsqrt(ms + EPS) * g_ref[...]).astype(o_ref.dtype)


def rmsnorm(x, g, out_dtype, *, col_block=0, width=None, tm=512):
    s = x.shape[0]
    width = x.shape[1] if width is None else width
    tm = min(tm, s)
    return pl.pallas_call(
        _rmsnorm_body,
        grid=(s // tm,),
        in_specs=[
            pl.BlockSpec((tm, width), lambda i: (i, col_block)),
            pl.BlockSpec((1, width), lambda i: (0, 0)),
        ],
        out_specs=pl.BlockSpec((tm, width), lambda i: (i, 0)),
        out_shape=jax.ShapeDtypeStruct((s, width), out_dtype),
        compiler_params=_params("parallel"),
        name="rmsnorm",
    )(x, g.reshape(1, width).astype(F32))


def _mm_body(a_ref, b_ref, *rest, epilogue, n_extra):
    acc = jnp.dot(a_ref[...], b_ref[...], preferred_element_type=F32)
    epilogue(acc, rest[:n_extra], rest[n_extra:])


def matmul(a, b, epilogue, extra, extra_specs, out_shapes, out_specs, *, tm, tn, name):
    m, k = a.shape
    n = b.shape[1]
    tm = min(tm, m)
    grid = (m // tm, n // tn)
    return pl.pallas_call(
        functools.partial(_mm_body, epilogue=epilogue, n_extra=len(extra)),
        grid=grid,
        in_specs=[
            pl.BlockSpec((tm, k), lambda i, j: (i, 0)),
            pl.BlockSpec((k, tn), lambda i, j: (0, j)),
            *extra_specs,
        ],
        out_specs=out_specs,
        out_shape=out_shapes,
        compiler_params=_params("parallel", "arbitrary"),
        name=name,
    )(a, b, *extra)


def _rope128_epilogue(acc, extra, outs):
    cos = extra[0][0]
    sin = extra[1][0]
    (o_ref,) = outs
    for c in range(acc.shape[1] // LANES):
        x = acc[:, c * LANES:(c + 1) * LANES]
        y = x * cos + pltpu.roll(x, LANES // 2, 1) * sin
        o_ref[:, c * LANES:(c + 1) * LANES] = y.astype(o_ref.dtype)


def _store_epilogue(acc, extra, outs):
    outs[0][...] = acc.astype(outs[0].dtype)


def _rope64_epilogue(acc, extra, outs):
    cos = extra[0][...]
    sin = extra[1][...]
    lane = lax.broadcasted_iota(jnp.int32, acc.shape, 1)
    half = QK_ROPE // 2
    rot = jnp.where(lane < half, pltpu.roll(acc, LANES - half, 1), pltpu.roll(acc, half, 1))
    outs[0][...] = (acc * cos + rot * sin).astype(outs[0].dtype)


def _sigmoid_epilogue(acc, extra, outs):
    z = acc + extra[0][...]
    outs[0][...] = (1.0 / (1.0 + jnp.exp(-z))).astype(outs[0].dtype)


def _residual_epilogue(acc, extra, outs):
    outs[0][...] = extra[0][...] + acc


def _uq_body(a_ref, w_ref, cos_ref, sin_ref, o_ref, *, scale):
    acc = jnp.dot(a_ref[...], w_ref[...], preferred_element_type=F32)
    half = QK_ROPE // 2
    r = acc[:, QK_NOPE:]
    rot = jnp.concatenate([r[:, half:], r[:, :half]], axis=1)
    r = r * cos_ref[...] + rot * sin_ref[...]
    o_ref[:, :QK_NOPE] = (acc[:, :QK_NOPE] * scale).astype(o_ref.dtype)
    o_ref[:, QK_NOPE:] = (r * scale).astype(o_ref.dtype)


def mla_q_proj(cqn, w_uq_h, cos64, sin64, *, tm=1024):
    s = cqn.shape[0]
    tm = min(tm, s)
    return pl.pallas_call(
        functools.partial(_uq_body, scale=QK_DIM_B ** -0.5 * math.log2(math.e)),
        grid=(s // tm, N_HEADS_B),
        in_specs=[
            pl.BlockSpec((tm, Q_LORA), lambda i, h: (i, 0)),
            pl.BlockSpec((None, Q_LORA, QK_DIM_B), lambda i, h: (h, 0, 0)),
            pl.BlockSpec((tm, QK_ROPE), lambda i, h: (i, 0)),
            pl.BlockSpec((tm, QK_ROPE), lambda i, h: (i, 0)),
        ],
        out_specs=pl.BlockSpec((None, tm, QK_DIM_B), lambda i, h: (h, i, 0)),
        out_shape=jax.ShapeDtypeStruct((N_HEADS_B, s, QK_DIM_B), BF16),
        compiler_params=_params("parallel", "arbitrary"),
        name="mla_q_proj",
    )(cqn, w_uq_h, cos64, sin64)


def _ukv_body(a_ref, w_ref, kr_ref, k_ref, v_ref):
    acc = jnp.dot(a_ref[...], w_ref[...], preferred_element_type=F32)
    k_ref[:, :QK_NOPE] = acc[:, :QK_NOPE].astype(k_ref.dtype)
    k_ref[:, QK_NOPE:] = kr_ref[:, :QK_ROPE].astype(k_ref.dtype)
    v_ref[...] = acc[:, QK_NOPE:].T.astype(v_ref.dtype)


def mla_kv_proj(ckvn, w_ukv, kr, *, tm=1024):
    s = ckvn.shape[0]
    tm = min(tm, s)
    hw = QK_NOPE + V_DIM
    return pl.pallas_call(
        _ukv_body,
        grid=(s // tm, N_HEADS_B),
        in_specs=[
            pl.BlockSpec((tm, KV_LORA), lambda i, h: (i, 0)),
            pl.BlockSpec((KV_LORA, hw), lambda i, h: (0, h)),
            pl.BlockSpec((tm, LANES), lambda i, h: (i, 0)),
        ],
        out_specs=[
            pl.BlockSpec((None, tm, QK_DIM_B), lambda i, h: (h, i, 0)),
            pl.BlockSpec((None, V_DIM, tm), lambda i, h: (h, 0, i)),
        ],
        out_shape=[
            jax.ShapeDtypeStruct((N_HEADS_B, s, QK_DIM_B), BF16),
            jax.ShapeDtypeStruct((N_HEADS_B, V_DIM, s), BF16),
        ],
        compiler_params=_params("parallel", "arbitrary"),
        name="mla_kv_proj",
    )(ckvn, w_ukv, kr)


def _flash_body(q_ref, k_ref, vt_ref, o_ref, s0_ref, s1_ref, acc_ref, *, tk):
    q = q_ref[...]
    tq = q.shape[0]
    n_kv = k_ref.shape[0] // tk

    def scores(j, s_ref):
        off = pl.multiple_of(j * tk, tk)
        k = k_ref[pl.ds(off, tk), :]
        st = lax.dot_general(k, q, (((1,), (1,)), ((), ())), preferred_element_type=F32)
        s_ref[...] = st
        return jnp.max(st, axis=0, keepdims=True)

    acc_ref[...] = jnp.zeros(acc_ref.shape, F32)

    def half(j, m, l, mx, cur_ref, nxt_ref):
        mx_next = scores(jnp.minimum(j + 1, n_kv - 1), nxt_ref)
        off = pl.multiple_of(j * tk, tk)
        vt = vt_ref[:, pl.ds(off, tk)]
        m_new = jnp.maximum(m, mx)
        alpha = jnp.exp2(m - m_new)
        p = jnp.exp2(cur_ref[...] - m_new)
        l = alpha * l + jnp.sum(p, axis=0, keepdims=True)
        acc_ref[...] = alpha * acc_ref[...] + jnp.dot(vt, p.astype(vt.dtype), preferred_element_type=F32)
        return m_new, l, mx_next

    def step(jj, carry):
        m, l, mx = carry
        m, l, mx = half(2 * jj, m, l, mx, s0_ref, s1_ref)
        m, l, mx = half(2 * jj + 1, m, l, mx, s1_ref, s0_ref)
        return m, l, mx

    m0 = jnp.full((1, tq), MASK_VALUE, F32)
    l0 = jnp.zeros((1, tq), F32)
    _, l, _ = lax.fori_loop(0, n_kv // 2, step, (m0, l0, scores(0, s0_ref)))
    o_ref[...] = (acc_ref[...] / l).T.astype(o_ref.dtype)


def dense_attention(q, k, vt, *, tq=512, tk=1024):
    h, s, dq = q.shape
    dv = vt.shape[1]
    tq = min(tq, s)
    tk = min(tk, s // 2)
    assert (s // tk) % 2 == 0
    return pl.pallas_call(
        functools.partial(_flash_body, tk=tk),
        grid=(h, s // tq),
        in_specs=[
            pl.BlockSpec((None, tq, dq), lambda hh, i: (hh, i, 0)),
            pl.BlockSpec((None, s, dq), lambda hh, i: (hh, 0, 0)),
            pl.BlockSpec((None, dv, s), lambda hh, i: (hh, 0, 0)),
        ],
        out_specs=pl.BlockSpec((tq, dv), lambda hh, i: (i, hh)),
        out_shape=jax.ShapeDtypeStruct((s, h * dv), BF16),
        scratch_shapes=[
            pltpu.VMEM((tk, tq), F32),
            pltpu.VMEM((tk, tq), F32),
            pltpu.VMEM((dv, tq), F32),
        ],
        compiler_params=_params("parallel", "arbitrary"),
        name="mla_flash",
    )(q, k, vt)


def _dilated_steps(t):
    steps = []
    for g, (window, d) in enumerate(DILATED_GROUPS):
        n = -(-(window // 2) // t)
        steps += [(g, o) for o in range(-n, n + 1)]
    return steps


def _step_lookup(step, values):
    out = jnp.int32(values[-1])
    for idx in range(len(values) - 2, -1, -1):
        out = jnp.where(step <= idx, jnp.int32(values[idx]), out)
    return out


def _dilated_body(q_ref, k_ref, v_ref, o_ref, m_sc, l_sc, acc_sc, *, t, groups, offsets, nblk):
    i = pl.program_id(0)
    step = pl.program_id(1)
    nsteps = pl.num_programs(1)

    @pl.when(step == 0)
    def _():
        m_sc[...] = jnp.full(m_sc.shape, MASK_VALUE, F32)
        l_sc[...] = jnp.zeros(l_sc.shape, F32)
        acc_sc[...] = jnp.zeros(acc_sc.shape, F32)

    off = _step_lookup(step, offsets)
    g = _step_lookup(step, groups)
    dil = _step_lookup(step, [DILATED_GROUPS[gg][1] for gg in groups])
    half = _step_lookup(step, [DILATED_GROUPS[gg][0] // 2 for gg in groups])
    kb = i + off

    @pl.when(jnp.logical_and(kb >= 0, kb < nblk))
    def _():
        row = lax.broadcasted_iota(jnp.int32, (t, t), 0)
        col = lax.broadcasted_iota(jnp.int32, (t, t), 1)
        dist = col - row + off * t
        valid = (jnp.abs(dist) <= half) & ((dist & (dil - 1)) == 0)
        for hh in range(HEADS_PER_GROUP_A):
            sl = slice(hh * HEAD_DIM_A, (hh + 1) * HEAD_DIM_A)
            q = q_ref[:, sl]
            k = k_ref[:, sl]
            v = v_ref[:, sl]
            s = lax.dot_general(q, k, (((1,), (1,)), ((), ())), preferred_element_type=F32)
            s = jnp.where(valid, s, MASK_VALUE)
            m_prev = m_sc[hh]
            m_new = jnp.maximum(m_prev, jnp.max(s, axis=1, keepdims=True))
            alpha = jnp.exp(m_prev - m_new)
            p = jnp.where(valid, jnp.exp(s - m_new), 0.0)
            l_sc[hh] = alpha * l_sc[hh] + jnp.sum(p, axis=1, keepdims=True)
            acc_sc[:, sl] = alpha * acc_sc[:, sl] + jnp.dot(p.astype(v.dtype), v, preferred_element_type=F32)
            m_sc[hh] = m_new

    @pl.when(step == nsteps - 1)
    def _():
        for hh in range(HEADS_PER_GROUP_A):
            sl = slice(hh * HEAD_DIM_A, (hh + 1) * HEAD_DIM_A)
            o_ref[:, sl] = (acc_sc[:, sl] / l_sc[hh]).astype(o_ref.dtype)


def dilated_attention(qk, v, *, t=256):
    s = qk.shape[0]
    t = min(t, s)
    nblk = s // t
    steps = _dilated_steps(t)
    groups = [g for g, _ in steps]
    offsets = [o for _, o in steps]
    n_g = len(DILATED_GROUPS)

    def q_map(i, st):
        return (i, _step_lookup(st, groups))

    def k_map(i, st):
        return (jnp.clip(i + _step_lookup(st, offsets), 0, nblk - 1), n_g + _step_lookup(st, groups))

    def v_map(i, st):
        return (jnp.clip(i + _step_lookup(st, offsets), 0, nblk - 1), _step_lookup(st, groups))

    return pl.pallas_call(
        functools.partial(_dilated_body, t=t, groups=groups, offsets=offsets, nblk=nblk),
        grid=(nblk, len(steps)),
        in_specs=[
            pl.BlockSpec((t, GROUP_WIDTH_A), q_map),
            pl.BlockSpec((t, GROUP_WIDTH_A), k_map),
            pl.BlockSpec((t, GROUP_WIDTH_A), v_map),
        ],
        out_specs=pl.BlockSpec((t, GROUP_WIDTH_A), lambda i, st: (i, 0)),
        out_shape=jax.ShapeDtypeStruct((s, GROUP_WIDTH_A), BF16),
        scratch_shapes=[
            pltpu.VMEM((HEADS_PER_GROUP_A, t, 1), F32),
            pltpu.VMEM((HEADS_PER_GROUP_A, t, 1), F32),
            pltpu.VMEM((t, GROUP_WIDTH_A), F32),
        ],
        compiler_params=_params("parallel", "arbitrary"),
        name="dilated_attn",
    )(qk, qk, v)


def _merge_body(oa_ref, ob_ref, woa_ref, wob_ref, g0_ref, g1_ref, o_ref):
    ya = jnp.dot(oa_ref[...], woa_ref[...], preferred_element_type=F32)
    yb = jnp.dot(ob_ref[...], wob_ref[...], preferred_element_type=F32)
    o_ref[...] = (g0_ref[...] * ya + g1_ref[...] * yb).astype(o_ref.dtype)


def gated_merge(oa, ob, w_oa, w_ob, gates, *, tm=1024, tn=512):
    s = oa.shape[0]
    d = w_oa.shape[1]
    tm = min(tm, s)
    nj = d // tn
    return pl.pallas_call(
        _merge_body,
        grid=(s // tm, nj),
        in_specs=[
            pl.BlockSpec((tm, oa.shape[1]), lambda i, j: (i, 0)),
            pl.BlockSpec((tm, ob.shape[1]), lambda i, j: (i, 0)),
            pl.BlockSpec((w_oa.shape[0], tn), lambda i, j: (0, j)),
            pl.BlockSpec((w_ob.shape[0], tn), lambda i, j: (0, j)),
            pl.BlockSpec((tm, tn), lambda i, j: (i, j)),
            pl.BlockSpec((tm, tn), lambda i, j: (i, j + nj)),
        ],
        out_specs=pl.BlockSpec((tm, tn), lambda i, j: (i, j)),
        out_shape=jax.ShapeDtypeStruct((s, d), BF16),
        compiler_params=_params("parallel", "arbitrary"),
        name="gated_merge",
    )(oa, ob, w_oa, w_ob, gates, gates)


HALO = 16


def _ffn_body(x_ref, xp_ref, xn_ref, g_ref, wa_ref, wb_ref, cwa_ref, cwb_ref, cba_ref, cbb_ref, wd_ref,
              o_ref, h_sc):
    i = pl.program_id(0)
    j = pl.program_id(1)
    tm = x_ref.shape[0]

    def norm(x):
        ms = jnp.mean(x * x, axis=-1, keepdims=True)
        return x * lax.rsqrt(ms + EPS) * g_ref[...]

    @pl.when(j == 0)
    def _():
        x = x_ref[...]
        o_ref[...] = x
        h_sc[HALO:HALO + tm, :] = norm(x).astype(h_sc.dtype)
        hp = jnp.where(i > 0, norm(xp_ref[...]), 0.0)
        hn = jnp.where(i < pl.num_programs(0) - 1, norm(xn_ref[...]), 0.0)
        h_sc[:HALO, :] = hp.astype(h_sc.dtype)
        h_sc[HALO + tm:, :] = hn.astype(h_sc.dtype)

    h = h_sc[...]
    rows = tm + 2 * HALO

    def conv(w_ref, cw_ref, cb_ref):
        up = jnp.dot(h, w_ref[...], preferred_element_type=F32)
        cw = cw_ref[...]
        prev = pltpu.roll(up, 1, 0)
        nxt = pltpu.roll(up, rows - 1, 0)
        u = cb_ref[...] + prev * cw[0:1] + up * cw[1:2] + nxt * cw[2:3]
        return u[HALO:HALO + tm]

    ua = conv(wa_ref, cwa_ref, cba_ref)
    ub = conv(wb_ref, cwb_ref, cbb_ref)
    act = (ua / (1.0 + jnp.exp(-ua))) * ub
    o_ref[...] += jnp.dot(act.astype(wd_ref.dtype), wd_ref[...], preferred_element_type=F32)


def conv_ffn(x, g, w_up, conv_w, conv_b, w_down, *, tm=512, tf=512):
    s, d = x.shape
    f = w_down.shape[0]
    tm = min(tm, s)
    nf = f // tf
    hb = tm // HALO
    n_halo_blocks = s // HALO
    return pl.pallas_call(
        _ffn_body,
        grid=(s // tm, nf),
        in_specs=[
            pl.BlockSpec((tm, d), lambda i, j: (i, 0)),
            pl.BlockSpec((HALO, d), lambda i, j: (jnp.maximum(i * hb - 1, 0), 0)),
            pl.BlockSpec((HALO, d), lambda i, j: (jnp.minimum((i + 1) * hb, n_halo_blocks - 1), 0)),
            pl.BlockSpec((1, d), lambda i, j: (0, 0)),
            pl.BlockSpec((d, tf), lambda i, j: (0, j)),
            pl.BlockSpec((d, tf), lambda i, j: (0, j + nf)),
            pl.BlockSpec((CONV_WIDTH, tf), lambda i, j: (0, j)),
            pl.BlockSpec((CONV_WIDTH, tf), lambda i, j: (0, j + nf)),
            pl.BlockSpec((1, tf), lambda i, j: (0, j)),
            pl.BlockSpec((1, tf), lambda i, j: (0, j + nf)),
            pl.BlockSpec((tf, d), lambda i, j: (j, 0)),
        ],
        out_specs=pl.BlockSpec((tm, d), lambda i, j: (i, 0)),
        out_shape=jax.ShapeDtypeStruct((s, d), F32),
        scratch_shapes=[pltpu.VMEM((tm + 2 * HALO, d), BF16)],
        compiler_params=_params("parallel", "arbitrary"),
        name="conv_ffn",
    )(x, x, x, g.reshape(1, d), w_up, w_up, conv_w, conv_w, conv_b.reshape(1, -1), conv_b.reshape(1, -1), w_down)


def _rope_tables(positions, dim):
    inv_freq = 1.0 / (ROPE_THETA ** (jnp.arange(0, dim, 2, dtype=F32) / dim))
    ang = positions.astype(F32)[:, None] * inv_freq
    cos = jnp.cos(ang)
    sin = jnp.sin(ang)
    return jnp.concatenate([cos, cos], axis=1), jnp.concatenate([-sin, sin], axis=1)


def kernel(x, positions, norm_mix, w_in, b_gate, norm_q, w_uq, norm_kv, w_ukv, w_oa, w_ob, w_out,
           norm_ffn, w_up, conv_w, conv_b, w_down, norm_final):
    batch, s, d = x.shape
    assert batch == 1
    depth = w_in.shape[0]
    xs = x.reshape(s, d)
    pos = positions.reshape(s)

    cos_a, sin_a = _rope_tables(pos, HEAD_DIM_A)
    scale_a = HEAD_DIM_A ** -0.5
    cos_qk = jnp.stack([cos_a * scale_a, cos_a])
    sin_qk = jnp.stack([sin_a * scale_a, sin_a])
    cos_b, sin_b = _rope_tables(pos, QK_ROPE)
    pad = ((0, 0), (0, LANES - QK_ROPE))
    cos_bp, sin_bp = jnp.pad(cos_b, pad), jnp.pad(sin_b, pad)

    o_ql = 3 * WIDTH_A
    o_kvl = o_ql + Q_LORA
    o_kr = o_kvl + KV_LORA
    o_g = o_kr + QK_ROPE
    tn = 512
    n_q_tiles = WIDTH_A // tn

    for l in range(depth):
        w_in_l = w_in[l].astype(BF16)
        w_qk = w_in_l[:, :2 * WIDTH_A]
        w_v = w_in_l[:, 2 * WIDTH_A:o_ql]
        w_lat = w_in_l[:, o_ql:o_kr]
        w_kr = jnp.pad(w_in_l[:, o_kr:o_g], ((0, 0), (0, LANES - QK_ROPE)))
        w_g = w_in_l[:, o_g:]

        h = rmsnorm(xs, norm_mix[l], BF16)
        tm = min(1024, s)

        qk = matmul(
            h, w_qk, _rope128_epilogue, (cos_qk, sin_qk),
            [pl.BlockSpec((1, tm, LANES), lambda i, j: (jnp.where(j < n_q_tiles, 0, 1), i, 0))] * 2,
            jax.ShapeDtypeStruct((s, 2 * WIDTH_A), BF16), pl.BlockSpec((tm, tn), lambda i, j: (i, j)),
            tm=tm, tn=tn, name="proj_qk")
        v = matmul(
            h, w_v, _store_epilogue, (), [],
            jax.ShapeDtypeStruct((s, WIDTH_A), BF16), pl.BlockSpec((tm, tn), lambda i, j: (i, j)),
            tm=tm, tn=tn, name="proj_v")
        lat = matmul(
            h, w_lat, _store_epilogue, (), [],
            jax.ShapeDtypeStruct((s, Q_LORA + KV_LORA), F32), pl.BlockSpec((tm, tn), lambda i, j: (i, j)),
            tm=tm, tn=tn, name="proj_lat")
        kr = matmul(
            h, w_kr, _rope64_epilogue, (cos_bp, sin_bp),
            [pl.BlockSpec((tm, LANES), lambda i, j: (i, 0))] * 2,
            jax.ShapeDtypeStruct((s, LANES), BF16), pl.BlockSpec((tm, LANES), lambda i, j: (i, j)),
            tm=tm, tn=LANES, name="proj_kr")
        gates = matmul(
            h, w_g, _sigmoid_epilogue, (b_gate[l].reshape(1, -1),),
            [pl.BlockSpec((1, tn), lambda i, j: (0, j))],
            jax.ShapeDtypeStruct((s, 2 * d), F32), pl.BlockSpec((tm, tn), lambda i, j: (i, j)),
            tm=tm, tn=tn, name="proj_gates")

        oa = dilated_attention(qk, v)

        cqn = rmsnorm(lat, norm_q[l], BF16, col_block=0, width=Q_LORA)
        ckvn = rmsnorm(lat, norm_kv[l], BF16, col_block=1, width=KV_LORA)
        w_uq_h = w_uq[l].astype(BF16).reshape(Q_LORA, N_HEADS_B, QK_DIM_B).transpose(1, 0, 2)
        q_b = mla_q_proj(cqn, w_uq_h, cos_b, sin_b)
        k_b, v_b = mla_kv_proj(ckvn, w_ukv[l].astype(BF16), kr)
        ob = dense_attention(q_b, k_b, v_b)

        merged = gated_merge(oa, ob, w_oa[l].astype(BF16), w_ob[l].astype(BF16), gates)
        xs = matmul(
            merged, w_out[l].astype(BF16), _residual_epilogue, (xs,),
            [pl.BlockSpec((tm, tn), lambda i, j: (i, j))],
            jax.ShapeDtypeStruct((s, d), F32), pl.BlockSpec((tm, tn), lambda i, j: (i, j)),
            tm=tm, tn=tn, name="proj_out")

        xs = conv_ffn(xs, norm_ffn[l], w_up[l].astype(BF16), conv_w[l], conv_b[l], w_down[l].astype(BF16))

    out = rmsnorm(xs, norm_final, F32)
    return out.reshape(batch, s, d)
```

```python
import functools
import math

import jax
import jax.numpy as jnp
from jax import lax
from jax.experimental import pallas as pl
from jax.experimental.pallas import tpu as pltpu

F32 = jnp.float32
BF16 = jnp.bfloat16

HEAD_DIM_A = 128
HEADS_PER_GROUP_A = 4
DILATED_GROUPS = ((128, 1), (512, 4), (2048, 16))
GROUP_WIDTH_A = HEADS_PER_GROUP_A * HEAD_DIM_A
WIDTH_A = len(DILATED_GROUPS) * GROUP_WIDTH_A
N_HEADS_B = 16
QK_NOPE = 128
QK_ROPE = 64
V_DIM = 128
Q_LORA = 512
KV_LORA = 512
QK_DIM_B = QK_NOPE + QK_ROPE
ROPE_THETA = 10000.0
EPS = 1e-6
CONV_WIDTH = 3

LANES = 128
MASK_VALUE = -1e30
VMEM_LIMIT = 56 * 1024 * 1024


def _params(*sem):
    return pltpu.CompilerParams(dimension_semantics=sem, vmem_limit_bytes=VMEM_LIMIT)


def _rmsnorm_body(x_ref, g_ref, o_ref):
    x = x_ref[...].astype(F32)
    ms = jnp.mean(x * x, axis=-1, keepdims=True)
    o_ref[...] = (x * lax.rsqrt(ms + EPS) * g_ref[...]).astype(o_ref.dtype)


def rmsnorm(x, g, out_dtype, *, tm=512):
    s, width = x.shape
    tm = min(tm, s)
    return pl.pallas_call(
        _rmsnorm_body,
        grid=(s // tm,),
        in_specs=[
            pl.BlockSpec((tm, width), lambda i: (i, 0)),
            pl.BlockSpec((1, width), lambda i: (0, 0)),
        ],
        out_specs=pl.BlockSpec((tm, width), lambda i: (i, 0)),
        out_shape=jax.ShapeDtypeStruct((s, width), out_dtype),
        compiler_params=_params("parallel"),
        name="rmsnorm",
    )(x, g.reshape(1, width).astype(F32))


def _mm_body(a_ref, b_ref, *rest, epilogue, n_extra):
    acc = jnp.dot(a_ref[...], b_ref[...], preferred_element_type=F32)
    epilogue(acc, rest[:n_extra], rest[n_extra:])


def matmul(a, b, epilogue, extra, extra_specs, out_shapes, out_specs, *, tm, tn, name):
    m, k = a.shape
    n = b.shape[1]
    tm = min(tm, m)
    grid = (m // tm, n // tn)
    return pl.pallas_call(
        functools.partial(_mm_body, epilogue=epilogue, n_extra=len(extra)),
        grid=grid,
        in_specs=[
            pl.BlockSpec((tm, k), lambda i, j: (i, 0)),
            pl.BlockSpec((k, tn), lambda i, j: (0, j)),
            *extra_specs,
        ],
        out_specs=out_specs,
        out_shape=out_shapes,
        compiler_params=_params("parallel", "arbitrary"),
        name=name,
    )(a, b, *extra)


def _rope128_epilogue(acc, extra, outs):
    cos = extra[0][0]
    sin = extra[1][0]
    (o_ref,) = outs
    for c in range(acc.shape[1] // LANES):
        x = acc[:, c * LANES:(c + 1) * LANES]
        y = x * cos + pltpu.roll(x, LANES // 2, 1) * sin
        o_ref[:, c * LANES:(c + 1) * LANES] = y.astype(o_ref.dtype)


def _store_epilogue(acc, extra, outs):
    outs[0][...] = acc.astype(outs[0].dtype)


def _store_transposed_epilogue(acc, extra, outs):
    outs[0][...] = acc.T.astype(outs[0].dtype)


def _rope64_epilogue(acc, extra, outs):
    cos = extra[0][...]
    sin = extra[1][...]
    lane = lax.broadcasted_iota(jnp.int32, acc.shape, 1)
    half = QK_ROPE // 2
    rot = jnp.where(lane < half, pltpu.roll(acc, LANES - half, 1), pltpu.roll(acc, half, 1))
    outs[0][...] = (acc * cos + rot * sin).astype(outs[0].dtype)


def _residual_epilogue(acc, extra, outs):
    outs[0][...] = extra[0][...] + acc


def _latent_norm_body(lat_ref, gq_ref, gkv_ref, cqt_ref, ckv_ref, ckvt_ref):
    def norm(x, g):
        ms = jnp.mean(x * x, axis=-1, keepdims=True)
        return x * lax.rsqrt(ms + EPS) * g

    cq = norm(lat_ref[:, :Q_LORA], gq_ref[...])
    ckv = norm(lat_ref[:, Q_LORA:], gkv_ref[...])
    cqt_ref[...] = cq.T.astype(cqt_ref.dtype)
    ckv_ref[...] = ckv.astype(ckv_ref.dtype)
    ckvt_ref[...] = ckv.T.astype(ckvt_ref.dtype)


def latent_norm(lat, g_q, g_kv, *, tm=512):
    s = lat.shape[0]
    tm = min(tm, s)
    return pl.pallas_call(
        _latent_norm_body,
        grid=(s // tm,),
        in_specs=[
            pl.BlockSpec((tm, Q_LORA + KV_LORA), lambda i: (i, 0)),
            pl.BlockSpec((1, Q_LORA), lambda i: (0, 0)),
            pl.BlockSpec((1, KV_LORA), lambda i: (0, 0)),
        ],
        out_specs=[
            pl.BlockSpec((Q_LORA, tm), lambda i: (0, i)),
            pl.BlockSpec((tm, KV_LORA), lambda i: (i, 0)),
            pl.BlockSpec((KV_LORA, tm), lambda i: (0, i)),
        ],
        out_shape=[
            jax.ShapeDtypeStruct((Q_LORA, s), BF16),
            jax.ShapeDtypeStruct((s, KV_LORA), BF16),
            jax.ShapeDtypeStruct((KV_LORA, s), BF16),
        ],
        compiler_params=_params("parallel"),
        name="latent_norm",
    )(lat, g_q.reshape(1, -1), g_kv.reshape(1, -1))


def _rope_rows_epilogue(acc, extra, outs, *, scale):
    cos = extra[0][...] * scale
    sin = extra[1][...] * scale
    (o_ref,) = outs
    half = QK_ROPE // 2
    for hh in range(acc.shape[0] // QK_DIM_B):
        r0 = hh * QK_DIM_B
        r1 = r0 + QK_NOPE
        x1 = acc[r1:r1 + half]
        x2 = acc[r1 + half:r0 + QK_DIM_B]
        o_ref[r0:r1, :] = (acc[r0:r1] * scale).astype(o_ref.dtype)
        o_ref[r1:r1 + half, :] = (x1 * cos - x2 * sin).astype(o_ref.dtype)
        o_ref[r1 + half:r0 + QK_DIM_B, :] = (x2 * cos + x1 * sin).astype(o_ref.dtype)


def _flash_body(q_ref, kn_ref, kr_ref, vt_ref, o_ref, k_ref, s0_ref, s1_ref, acc_ref, *, tk):
    @pl.when(pl.program_id(1) == 0)
    def _():
        k_ref[:, :QK_NOPE] = kn_ref[...]
        k_ref[:, QK_NOPE:] = kr_ref[:, :QK_ROPE]

    qt = q_ref[...]
    tq = qt.shape[1]
    n_kv = k_ref.shape[0] // tk

    def scores(j, s_ref):
        off = pl.multiple_of(j * tk, tk)
        k = k_ref[pl.ds(off, tk), :]
        st = jnp.dot(k, qt, preferred_element_type=F32)
        s_ref[...] = st
        return jnp.max(st, axis=0, keepdims=True)

    acc_ref[...] = jnp.zeros(acc_ref.shape, F32)

    def half(j, m, l, mx, cur_ref, nxt_ref):
        mx_next = scores(jnp.minimum(j + 1, n_kv - 1), nxt_ref)
        off = pl.multiple_of(j * tk, tk)
        vt = vt_ref[:, pl.ds(off, tk)]
        m_new = jnp.maximum(m, mx)
        alpha = jnp.exp2(m - m_new)
        p = jnp.exp2(cur_ref[...] - m_new)
        l = alpha * l + jnp.sum(p, axis=0, keepdims=True)
        acc_ref[...] = alpha * acc_ref[...] + jnp.dot(vt, p.astype(vt.dtype), preferred_element_type=F32)
        return m_new, l, mx_next

    def step(jj, carry):
        m, l, mx = carry
        m, l, mx = half(2 * jj, m, l, mx, s0_ref, s1_ref)
        m, l, mx = half(2 * jj + 1, m, l, mx, s1_ref, s0_ref)
        return m, l, mx

    m0 = jnp.full((1, tq), MASK_VALUE, F32)
    l0 = jnp.zeros((1, tq), F32)
    _, l, _ = lax.fori_loop(0, n_kv // 2, step, (m0, l0, scores(0, s0_ref)))
    o_ref[...] = (acc_ref[...] / l).T.astype(o_ref.dtype)


def dense_attention(qt, k_nope, kr, vt, *, tq=512, tk=1024):
    h, dq, s = qt.shape
    dv = vt.shape[1]
    tq = min(tq, s)
    tk = min(tk, s // 2)
    assert (s // tk) % 2 == 0
    return pl.pallas_call(
        functools.partial(_flash_body, tk=tk),
        grid=(h, s // tq),
        in_specs=[
            pl.BlockSpec((None, dq, tq), lambda hh, i: (hh, 0, i)),
            pl.BlockSpec((s, QK_NOPE), lambda hh, i: (0, hh)),
            pl.BlockSpec((s, LANES), lambda hh, i: (0, 0)),
            pl.BlockSpec((None, dv, s), lambda hh, i: (hh, 0, 0)),
        ],
        out_specs=pl.BlockSpec((tq, dv), lambda hh, i: (i, hh)),
        out_shape=jax.ShapeDtypeStruct((s, h * dv), BF16),
        scratch_shapes=[
            pltpu.VMEM((s, dq), BF16),
            pltpu.VMEM((tk, tq), F32),
            pltpu.VMEM((tk, tq), F32),
            pltpu.VMEM((dv, tq), F32),
        ],
        compiler_params=_params("arbitrary", "arbitrary"),
        name="mla_flash",
    )(qt, k_nope, kr, vt)


def _dilated_steps(t):
    steps = []
    for g, (window, d) in enumerate(DILATED_GROUPS):
        n = -(-(window // 2) // t)
        steps += [(g, 0)] + [(g, o) for o in range(-n, n + 1) if o != 0]
    return steps


def _step_lookup(step, values):
    out = jnp.int32(values[-1])
    for idx in range(len(values) - 2, -1, -1):
        out = jnp.where(step <= idx, jnp.int32(values[idx]), out)
    return out


def _dilated_body(q_ref, k_ref, vt_ref, o_ref, m_sc, l_sc, acc_sc, s_sc, *, t, groups, offsets, nblk):
    i = pl.program_id(0)
    step = pl.program_id(1)
    nsteps = pl.num_programs(1)

    @pl.when(step == 0)
    def _():
        m_sc[...] = jnp.full(m_sc.shape, MASK_VALUE, F32)
        l_sc[...] = jnp.zeros(l_sc.shape, F32)
        acc_sc[...] = jnp.zeros(acc_sc.shape, F32)

    off = _step_lookup(step, offsets)
    dil = _step_lookup(step, [DILATED_GROUPS[gg][1] for gg in groups])
    half = _step_lookup(step, [DILATED_GROUPS[gg][0] // 2 for gg in groups])
    kb = i + off

    @pl.when(jnp.logical_and(kb >= 0, kb < nblk))
    def _():
        key = lax.broadcasted_iota(jnp.int32, (t, t), 0)
        qry = lax.broadcasted_iota(jnp.int32, (t, t), 1)
        dist = key - qry + off * t
        valid = (jnp.abs(dist) <= half) & ((dist & (dil - 1)) == 0)
        for hh in range(HEADS_PER_GROUP_A):
            sl = slice(hh * HEAD_DIM_A, (hh + 1) * HEAD_DIM_A)
            st = lax.dot_general(k_ref[:, sl], q_ref[:, sl], (((1,), (1,)), ((), ())),
                                 preferred_element_type=F32)
            s_sc[hh] = jnp.where(valid, st, MASK_VALUE)
        for hh in range(HEADS_PER_GROUP_A):
            sl = slice(hh * HEAD_DIM_A, (hh + 1) * HEAD_DIM_A)
            st = s_sc[hh]
            m_prev = m_sc[hh]
            m_new = jnp.maximum(m_prev, jnp.max(st, axis=0, keepdims=True))
            alpha = jnp.exp2(m_prev - m_new)
            p = jnp.exp2(st - m_new)
            l_sc[hh] = alpha * l_sc[hh] + jnp.sum(p, axis=0, keepdims=True)
            vt = vt_ref[sl, :]
            acc_sc[sl, :] = alpha * acc_sc[sl, :] + jnp.dot(vt, p.astype(vt.dtype), preferred_element_type=F32)
            m_sc[hh] = m_new

    @pl.when(step == nsteps - 1)
    def _():
        for hh in range(HEADS_PER_GROUP_A):
            sl = slice(hh * HEAD_DIM_A, (hh + 1) * HEAD_DIM_A)
            o_ref[:, sl] = (acc_sc[sl, :] / l_sc[hh]).T.astype(o_ref.dtype)


def dilated_attention(qk, vt, *, t=256):
    s = qk.shape[0]
    t = min(t, s)
    nblk = s // t
    steps = _dilated_steps(t)
    groups = [g for g, _ in steps]
    offsets = [o for _, o in steps]
    n_g = len(DILATED_GROUPS)

    def q_map(i, st):
        return (i, _step_lookup(st, groups))

    def k_map(i, st):
        return (jnp.clip(i + _step_lookup(st, offsets), 0, nblk - 1), n_g + _step_lookup(st, groups))

    def v_map(i, st):
        return (_step_lookup(st, groups), jnp.clip(i + _step_lookup(st, offsets), 0, nblk - 1))

    return pl.pallas_call(
        functools.partial(_dilated_body, t=t, groups=groups, offsets=offsets, nblk=nblk),
        grid=(nblk, len(steps)),
        in_specs=[
            pl.BlockSpec((t, GROUP_WIDTH_A), q_map),
            pl.BlockSpec((t, GROUP_WIDTH_A), k_map),
            pl.BlockSpec((GROUP_WIDTH_A, t), v_map),
        ],
        out_specs=pl.BlockSpec((t, GROUP_WIDTH_A), lambda i, st: (i, 0)),
        out_shape=jax.ShapeDtypeStruct((s, GROUP_WIDTH_A), BF16),
        scratch_shapes=[
            pltpu.VMEM((HEADS_PER_GROUP_A, 1, t), F32),
            pltpu.VMEM((HEADS_PER_GROUP_A, 1, t), F32),
            pltpu.VMEM((GROUP_WIDTH_A, t), F32),
            pltpu.VMEM((HEADS_PER_GROUP_A, t, t), F32),
        ],
        compiler_params=_params("parallel", "arbitrary"),
        name="dilated_attn",
    )(qk, qk, vt)


def _merge_body(oa_ref, ob_ref, h_ref, woa_ref, wob_ref, wga_ref, wgb_ref, ba_ref, bb_ref, o_ref):
    h = h_ref[...]

    def gate(wg_ref, b_ref):
        z = jnp.dot(h, wg_ref[...], preferred_element_type=F32) + b_ref[...]
        return 1.0 / (1.0 + jnp.exp(-z))

    ya = jnp.dot(oa_ref[...], woa_ref[...], preferred_element_type=F32)
    yb = jnp.dot(ob_ref[...], wob_ref[...], preferred_element_type=F32)
    o_ref[...] = (gate(wga_ref, ba_ref) * ya + gate(wgb_ref, bb_ref) * yb).astype(o_ref.dtype)


def gated_merge(oa, ob, h, w_oa, w_ob, w_g, b_g, *, tm=1024, tn=512):
    s = oa.shape[0]
    d = w_oa.shape[1]
    tm = min(tm, s)
    nj = d // tn
    return pl.pallas_call(
        _merge_body,
        grid=(s // tm, nj),
        in_specs=[
            pl.BlockSpec((tm, oa.shape[1]), lambda i, j: (i, 0)),
            pl.BlockSpec((tm, ob.shape[1]), lambda i, j: (i, 0)),
            pl.BlockSpec((tm, h.shape[1]), lambda i, j: (i, 0)),
            pl.BlockSpec((w_oa.shape[0], tn), lambda i, j: (0, j)),
            pl.BlockSpec((w_ob.shape[0], tn), lambda i, j: (0, j)),
            pl.BlockSpec((w_g.shape[0], tn), lambda i, j: (0, j)),
            pl.BlockSpec((w_g.shape[0], tn), lambda i, j: (0, j + nj)),
            pl.BlockSpec((1, tn), lambda i, j: (0, j)),
            pl.BlockSpec((1, tn), lambda i, j: (0, j + nj)),
        ],
        out_specs=pl.BlockSpec((tm, tn), lambda i, j: (i, j)),
        out_shape=jax.ShapeDtypeStruct((s, d), BF16),
        compiler_params=_params("parallel", "arbitrary"),
        name="gated_merge",
    )(oa, ob, h, w_oa, w_ob, w_g, w_g, b_g, b_g)


HALO = 16


def _ffn_body(x_ref, xp_ref, xn_ref, g_ref, wa_ref, wb_ref, cwa_ref, cwb_ref, cba_ref, cbb_ref, wd_ref,
              o_ref, h_sc):
    i = pl.program_id(0)
    j = pl.program_id(1)
    tm = x_ref.shape[0]

    def norm(x):
        ms = jnp.mean(x * x, axis=-1, keepdims=True)
        return x * lax.rsqrt(ms + EPS) * g_ref[...]

    @pl.when(j == 0)
    def _():
        x = x_ref[...]
        o_ref[...] = x
        h_sc[HALO:HALO + tm, :] = norm(x).astype(h_sc.dtype)
        hp = jnp.where(i > 0, norm(xp_ref[...]), 0.0)
        hn = jnp.where(i < pl.num_programs(0) - 1, norm(xn_ref[...]), 0.0)
        h_sc[:HALO, :] = hp.astype(h_sc.dtype)
        h_sc[HALO + tm:, :] = hn.astype(h_sc.dtype)

    h = h_sc[...]
    rows = tm + 2 * HALO

    def conv(w_ref, cw_ref, cb_ref):
        up = jnp.dot(h, w_ref[...], preferred_element_type=F32)
        cw = cw_ref[...]
        prev = pltpu.roll(up, 1, 0)
        nxt = pltpu.roll(up, rows - 1, 0)
        u = cb_ref[...] + prev * cw[0:1] + up * cw[1:2] + nxt * cw[2:3]
        return u[HALO:HALO + tm]

    ua = conv(wa_ref, cwa_ref, cba_ref)
    ub = conv(wb_ref, cwb_ref, cbb_ref)
    act = (ua / (1.0 + jnp.exp(-ua))) * ub
    o_ref[...] += jnp.dot(act.astype(wd_ref.dtype), wd_ref[...], preferred_element_type=F32)


def conv_ffn(x, g, w_up, conv_w, conv_b, w_down, *, tm=1024, tf=512):
    s, d = x.shape
    f = w_down.shape[0]
    tm = min(tm, s)
    nf = f // tf
    hb = tm // HALO
    n_halo_blocks = s // HALO
    return pl.pallas_call(
        _ffn_body,
        grid=(s // tm, nf),
        in_specs=[
            pl.BlockSpec((tm, d), lambda i, j: (i, 0)),
            pl.BlockSpec((HALO, d), lambda i, j: (jnp.maximum(i * hb - 1, 0), 0)),
            pl.BlockSpec((HALO, d), lambda i, j: (jnp.minimum((i + 1) * hb, n_halo_blocks - 1), 0)),
            pl.BlockSpec((1, d), lambda i, j: (0, 0)),
            pl.BlockSpec((d, tf), lambda i, j: (0, j)),
            pl.BlockSpec((d, tf), lambda i, j: (0, j + nf)),
            pl.BlockSpec((CONV_WIDTH, tf), lambda i, j: (0, j)),
            pl.BlockSpec((CONV_WIDTH, tf), lambda i, j: (0, j + nf)),
            pl.BlockSpec((1, tf), lambda i, j: (0, j)),
            pl.BlockSpec((1, tf), lambda i, j: (0, j + nf)),
            pl.BlockSpec((tf, d), lambda i, j: (j, 0)),
        ],
        out_specs=pl.BlockSpec((tm, d), lambda i, j: (i, 0)),
        out_shape=jax.ShapeDtypeStruct((s, d), F32),
        scratch_shapes=[pltpu.VMEM((tm + 2 * HALO, d), BF16)],
        compiler_params=_params("parallel", "arbitrary"),
        name="conv_ffn",
    )(x, x, x, g.reshape(1, d), w_up, w_up, conv_w, conv_w, conv_b.reshape(1, -1), conv_b.reshape(1, -1), w_down)


def _rope_tables(positions, dim):
    inv_freq = 1.0 / (ROPE_THETA ** (jnp.arange(0, dim, 2, dtype=F32) / dim))
    ang = positions.astype(F32)[:, None] * inv_freq
    cos = jnp.cos(ang)
    sin = jnp.sin(ang)
    return jnp.concatenate([cos, cos], axis=1), jnp.concatenate([-sin, sin], axis=1)


def kernel(x, positions, norm_mix, w_in, b_gate, norm_q, w_uq, norm_kv, w_ukv, w_oa, w_ob, w_out,
           norm_ffn, w_up, conv_w, conv_b, w_down, norm_final):
    batch, s, d = x.shape
    assert batch == 1
    depth = w_in.shape[0]
    xs = x.reshape(s, d)
    pos = positions.reshape(s)

    cos_a, sin_a = _rope_tables(pos, HEAD_DIM_A)
    scale_a = HEAD_DIM_A ** -0.5 * math.log2(math.e)
    cos_qk = jnp.stack([cos_a * scale_a, cos_a])
    sin_qk = jnp.stack([sin_a * scale_a, sin_a])
    cos_b, sin_b = _rope_tables(pos, QK_ROPE)
    pad = ((0, 0), (0, LANES - QK_ROPE))
    cos_bp, sin_bp = jnp.pad(cos_b, pad), jnp.pad(sin_b, pad)
    cos_bt = cos_b[:, QK_ROPE // 2:].T
    sin_bt = sin_b[:, QK_ROPE // 2:].T
    scale_b = QK_DIM_B ** -0.5 * math.log2(math.e)

    o_ql = 3 * WIDTH_A
    o_kvl = o_ql + Q_LORA
    o_kr = o_kvl + KV_LORA
    o_g = o_kr + QK_ROPE
    tn = 512
    n_q_tiles = WIDTH_A // tn

    for l in range(depth):
        w_in_l = w_in[l].astype(BF16)
        w_qk = w_in_l[:, :2 * WIDTH_A]
        w_v = w_in_l[:, 2 * WIDTH_A:o_ql]
        w_lat = w_in_l[:, o_ql:o_kr]
        w_kr = jnp.pad(w_in_l[:, o_kr:o_g], ((0, 0), (0, LANES - QK_ROPE)))
        w_g = w_in_l[:, o_g:]

        h = rmsnorm(xs, norm_mix[l], BF16)
        tm = min(1024, s)

        qk = matmul(
            h, w_qk, _rope128_epilogue, (cos_qk, sin_qk),
            [pl.BlockSpec((1, tm, LANES), lambda i, j: (jnp.where(j < n_q_tiles, 0, 1), i, 0))] * 2,
            jax.ShapeDtypeStruct((s, 2 * WIDTH_A), BF16), pl.BlockSpec((tm, tn), lambda i, j: (i, j)),
            tm=tm, tn=tn, name="proj_qk")
        vt = matmul(
            h, w_v, _store_transposed_epilogue, (), [],
            jax.ShapeDtypeStruct((WIDTH_A, s), BF16), pl.BlockSpec((tn, tm), lambda i, j: (j, i)),
            tm=tm, tn=tn, name="proj_v")
        lat = matmul(
            h, w_lat, _store_epilogue, (), [],
            jax.ShapeDtypeStruct((s, Q_LORA + KV_LORA), F32), pl.BlockSpec((tm, tn), lambda i, j: (i, j)),
            tm=tm, tn=tn, name="proj_lat")
        kr = matmul(
            h, w_kr, _rope64_epilogue, (cos_bp, sin_bp),
            [pl.BlockSpec((tm, LANES), lambda i, j: (i, 0))] * 2,
            jax.ShapeDtypeStruct((s, LANES), BF16), pl.BlockSpec((tm, LANES), lambda i, j: (i, j)),
            tm=tm, tn=LANES, name="proj_kr")
        oa = dilated_attention(qk, vt)

        cq_t, ckv, ckv_t = latent_norm(lat, norm_q[l], norm_kv[l])
        w_uq_t = w_uq[l].T.astype(BF16)
        w_ukv_l = w_ukv[l].reshape(KV_LORA, N_HEADS_B, QK_NOPE + V_DIM)
        w_uk = w_ukv_l[:, :, :QK_NOPE].reshape(KV_LORA, N_HEADS_B * QK_NOPE).astype(BF16)
        w_uv_t = w_ukv_l[:, :, QK_NOPE:].reshape(KV_LORA, N_HEADS_B * V_DIM).T.astype(BF16)
        heads_per_tile = 4
        tq_rows = heads_per_tile * QK_DIM_B
        qt_b = matmul(
            w_uq_t, cq_t, functools.partial(_rope_rows_epilogue, scale=scale_b), (cos_bt, sin_bt),
            [pl.BlockSpec((QK_ROPE // 2, tm), lambda i, j: (0, j))] * 2,
            jax.ShapeDtypeStruct((N_HEADS_B * QK_DIM_B, s), BF16), pl.BlockSpec((tq_rows, tm), lambda i, j: (i, j)),
            tm=tq_rows, tn=tm, name="mla_q_proj").reshape(N_HEADS_B, QK_DIM_B, s)
        vt_b = matmul(
            w_uv_t, ckv_t, _store_epilogue, (), [],
            jax.ShapeDtypeStruct((N_HEADS_B * V_DIM, s), BF16), pl.BlockSpec((tm, tm), lambda i, j: (i, j)),
            tm=tm, tn=tm, name="mla_v_proj").reshape(N_HEADS_B, V_DIM, s)
        k_nope = matmul(
            ckv, w_uk, _store_epilogue, (), [],
            jax.ShapeDtypeStruct((s, N_HEADS_B * QK_NOPE), BF16), pl.BlockSpec((tm, tn), lambda i, j: (i, j)),
            tm=tm, tn=tn, name="mla_k_proj")
        ob = dense_attention(qt_b, k_nope, kr, vt_b)

        merged = gated_merge(oa, ob, h, w_oa[l].astype(BF16), w_ob[l].astype(BF16), w_g,
                             b_gate[l].reshape(1, -1))
        xs = matmul(
            merged, w_out[l].astype(BF16), _residual_epilogue, (xs,),
            [pl.BlockSpec((tm, tn), lambda i, j: (i, j))],
            jax.ShapeDtypeStruct((s, d), F32), pl.BlockSpec((tm, tn), lambda i, j: (i, j)),
            tm=tm, tn=tn, name="proj_out")

        xs = conv_ffn(xs, norm_ffn[l], w_up[l].astype(BF16), conv_w[l], conv_b[l], w_down[l].astype(BF16))

    out = rmsnorm(xs, norm_final, F32)
    return out.reshape(batch, s, d)
```

```python
import functools
import math

import jax
import jax.numpy as jnp
from jax import lax
from jax.experimental import pallas as pl
from jax.experimental.pallas import tpu as pltpu

F32 = jnp.float32
BF16 = jnp.bfloat16

HEAD_DIM_A = 128
HEADS_PER_GROUP_A = 4
DILATED_GROUPS = ((128, 1), (512, 4), (2048, 16))
GROUP_WIDTH_A = HEADS_PER_GROUP_A * HEAD_DIM_A
WIDTH_A = len(DILATED_GROUPS) * GROUP_WIDTH_A
N_HEADS_B = 16
QK_NOPE = 128
QK_ROPE = 64
V_DIM = 128
Q_LORA = 512
KV_LORA = 512
QK_DIM_B = QK_NOPE + QK_ROPE
ROPE_THETA = 10000.0
EPS = 1e-6
CONV_WIDTH = 3

LANES = 128
MASK_VALUE = -1e30
VMEM_LIMIT = 56 * 1024 * 1024


def _params(*sem):
    return pltpu.CompilerParams(dimension_semantics=sem, vmem_limit_bytes=VMEM_LIMIT)


def _rmsnorm_body(x_ref, g_ref, o_ref):
    x = x_ref[...].astype(F32)
    ms = jnp.mean(x * x, axis=-1, keepdims=True)
    o_ref[...] = (x * lax.rsqrt(ms + EPS) * g_ref[...]).astype(o_ref.dtype)


def rmsnorm(x, g, out_dtype, *, tm=512):
    s, width = x.shape
    tm = min(tm, s)
    return pl.pallas_call(
        _rmsnorm_body,
        grid=(s // tm,),
        in_specs=[
            pl.BlockSpec((tm, width), lambda i: (i, 0)),
            pl.BlockSpec((1, width), lambda i: (0, 0)),
        ],
        out_specs=pl.BlockSpec((tm, width), lambda i: (i, 0)),
        out_shape=jax.ShapeDtypeStruct((s, width), out_dtype),
        compiler_params=_params("parallel"),
        name="rmsnorm",
    )(x, g.reshape(1, width).astype(F32))


def _mm_body(a_ref, b_ref, *rest, epilogue, n_extra):
    acc = jnp.dot(a_ref[...], b_ref[...], preferred_element_type=F32)
    epilogue(acc, rest[:n_extra], rest[n_extra:])


def matmul(a, b, epilogue, extra, extra_specs, out_shapes, out_specs, *, tm, tn, name):
    m, k = a.shape
    n = b.shape[1]
    tm = min(tm, m)
    grid = (m // tm, n // tn)
    return pl.pallas_call(
        functools.partial(_mm_body, epilogue=epilogue, n_extra=len(extra)),
        grid=grid,
        in_specs=[
            pl.BlockSpec((tm, k), lambda i, j: (i, 0)),
            pl.BlockSpec((k, tn), lambda i, j: (0, j)),
            *extra_specs,
        ],
        out_specs=out_specs,
        out_shape=out_shapes,
        compiler_params=_params("parallel", "arbitrary"),
        name=name,
    )(a, b, *extra)


def _rope128_epilogue(acc, extra, outs, *, transposed=False):
    cos = extra[0][0]
    sin = extra[1][0]
    (o_ref,) = outs
    for c in range(acc.shape[1] // LANES):
        x = acc[:, c * LANES:(c + 1) * LANES]
        y = x * cos + pltpu.roll(x, LANES // 2, 1) * sin
        if transposed:
            o_ref[c * LANES:(c + 1) * LANES, :] = y.T.astype(o_ref.dtype)
        else:
            o_ref[:, c * LANES:(c + 1) * LANES] = y.astype(o_ref.dtype)


def _store_epilogue(acc, extra, outs):
    outs[0][...] = acc.astype(outs[0].dtype)


def _rope64_epilogue(acc, extra, outs):
    cos = extra[0][...]
    sin = extra[1][...]
    lane = lax.broadcasted_iota(jnp.int32, acc.shape, 1)
    half = QK_ROPE // 2
    rot = jnp.where(lane < half, pltpu.roll(acc, LANES - half, 1), pltpu.roll(acc, half, 1))
    outs[0][...] = (acc * cos + rot * sin).astype(outs[0].dtype)


def _residual_epilogue(acc, extra, outs):
    outs[0][...] = extra[0][...] + acc


IN_TN = 512
N_A_TILES = WIDTH_A // IN_TN
N_LAT_TILES = (Q_LORA + KV_LORA) // IN_TN


def _in_proj_body(x_ref, g_ref, w_ref, wkr_ref, cosa_ref, sina_ref, cosb_ref, sinb_ref,
                  qt_ref, k_ref, vt_ref, lat_ref, kr_ref, h_ref):
    j = pl.program_id(1)

    @pl.when(j == 0)
    def _():
        x = x_ref[...]
        ms = jnp.mean(x * x, axis=-1, keepdims=True)
        h = (x * lax.rsqrt(ms + EPS) * g_ref[...]).astype(h_ref.dtype)
        h_ref[...] = h
        _rope64_epilogue(jnp.dot(h, wkr_ref[...], preferred_element_type=F32), (cosb_ref, sinb_ref), (kr_ref,))

    acc = jnp.dot(h_ref[...], w_ref[...], preferred_element_type=F32)

    @pl.when(j < N_A_TILES)
    def _():
        _rope128_epilogue(acc, (cosa_ref, sina_ref), (qt_ref,), transposed=True)

    @pl.when(jnp.logical_and(j >= N_A_TILES, j < 2 * N_A_TILES))
    def _():
        _rope128_epilogue(acc, (cosa_ref, sina_ref), (k_ref,))

    @pl.when(jnp.logical_and(j >= 2 * N_A_TILES, j < 3 * N_A_TILES))
    def _():
        vt_ref[...] = acc.T.astype(vt_ref.dtype)

    @pl.when(j >= 3 * N_A_TILES)
    def _():
        lat_ref[...] = acc


def in_proj(x, g, w_in_l, w_kr, cos_qk, sin_qk, cos_bp, sin_bp, *, tm=1024):
    s, d = x.shape
    tm = min(tm, s)
    tn = IN_TN
    n_a = N_A_TILES
    table = pl.BlockSpec((1, tm, LANES), lambda i, j: (jnp.where(j < n_a, 0, 1), i, 0))
    return pl.pallas_call(
        _in_proj_body,
        grid=(s // tm, 3 * n_a + N_LAT_TILES),
        in_specs=[
            pl.BlockSpec((tm, d), lambda i, j: (i, 0)),
            pl.BlockSpec((1, d), lambda i, j: (0, 0)),
            pl.BlockSpec((d, tn), lambda i, j: (0, j)),
            pl.BlockSpec((d, LANES), lambda i, j: (0, 0)),
            table,
            table,
            pl.BlockSpec((tm, LANES), lambda i, j: (i, 0)),
            pl.BlockSpec((tm, LANES), lambda i, j: (i, 0)),
        ],
        out_specs=[
            pl.BlockSpec((tn, tm), lambda i, j: (jnp.minimum(j, n_a - 1), i)),
            pl.BlockSpec((tm, tn), lambda i, j: (i, jnp.clip(j - n_a, 0, n_a - 1))),
            pl.BlockSpec((tn, tm), lambda i, j: (jnp.clip(j - 2 * n_a, 0, n_a - 1), i)),
            pl.BlockSpec((tm, tn), lambda i, j: (i, jnp.clip(j - 3 * n_a, 0, N_LAT_TILES - 1))),
            pl.BlockSpec((tm, LANES), lambda i, j: (i, 0)),
            pl.BlockSpec((tm, d), lambda i, j: (i, 0)),
        ],
        out_shape=[
            jax.ShapeDtypeStruct((WIDTH_A, s), BF16),
            jax.ShapeDtypeStruct((s, WIDTH_A), BF16),
            jax.ShapeDtypeStruct((WIDTH_A, s), BF16),
            jax.ShapeDtypeStruct((s, Q_LORA + KV_LORA), F32),
            jax.ShapeDtypeStruct((s, LANES), BF16),
            jax.ShapeDtypeStruct((s, d), BF16),
        ],
        compiler_params=_params("arbitrary", "arbitrary"),
        name="in_proj",
    )(x, g.reshape(1, d), w_in_l, w_kr, cos_qk, sin_qk, cos_bp, sin_bp)


def _latent_norm_body(lat_ref, gq_ref, gkv_ref, cqt_ref, ckv_ref, ckvt_ref):
    def norm(x, g):
        ms = jnp.mean(x * x, axis=-1, keepdims=True)
        return x * lax.rsqrt(ms + EPS) * g

    cq = norm(lat_ref[:, :Q_LORA], gq_ref[...])
    ckv = norm(lat_ref[:, Q_LORA:], gkv_ref[...])
    cqt_ref[...] = cq.T.astype(cqt_ref.dtype)
    ckv_ref[...] = ckv.astype(ckv_ref.dtype)
    ckvt_ref[...] = ckv.T.astype(ckvt_ref.dtype)


def latent_norm(lat, g_q, g_kv, *, tm=512):
    s = lat.shape[0]
    tm = min(tm, s)
    return pl.pallas_call(
        _latent_norm_body,
        grid=(s // tm,),
        in_specs=[
            pl.BlockSpec((tm, Q_LORA + KV_LORA), lambda i: (i, 0)),
            pl.BlockSpec((1, Q_LORA), lambda i: (0, 0)),
            pl.BlockSpec((1, KV_LORA), lambda i: (0, 0)),
        ],
        out_specs=[
            pl.BlockSpec((Q_LORA, tm), lambda i: (0, i)),
            pl.BlockSpec((tm, KV_LORA), lambda i: (i, 0)),
            pl.BlockSpec((KV_LORA, tm), lambda i: (0, i)),
        ],
        out_shape=[
            jax.ShapeDtypeStruct((Q_LORA, s), BF16),
            jax.ShapeDtypeStruct((s, KV_LORA), BF16),
            jax.ShapeDtypeStruct((KV_LORA, s), BF16),
        ],
        compiler_params=_params("parallel"),
        name="latent_norm",
    )(lat, g_q.reshape(1, -1), g_kv.reshape(1, -1))


def _rope_rows_epilogue(acc, extra, outs, *, scale):
    cos = extra[0][...] * scale
    sin = extra[1][...] * scale
    (o_ref,) = outs
    half = QK_ROPE // 2
    for hh in range(acc.shape[0] // QK_DIM_B):
        r0 = hh * QK_DIM_B
        r1 = r0 + QK_NOPE
        x1 = acc[r1:r1 + half]
        x2 = acc[r1 + half:r0 + QK_DIM_B]
        o_ref[r0:r1, :] = (acc[r0:r1] * scale).astype(o_ref.dtype)
        o_ref[r1:r1 + half, :] = (x1 * cos - x2 * sin).astype(o_ref.dtype)
        o_ref[r1 + half:r0 + QK_DIM_B, :] = (x2 * cos + x1 * sin).astype(o_ref.dtype)


def _flash_body(q_ref, kn_ref, kr_ref, vt_ref, o_ref, k_ref, s0_ref, s1_ref, acc_ref, *, tk):
    @pl.when(pl.program_id(1) == 0)
    def _():
        k_ref[:, :QK_NOPE] = kn_ref[...]
        k_ref[:, QK_NOPE:] = kr_ref[:, :QK_ROPE]

    qt = q_ref[...]
    tq = qt.shape[1]
    n_kv = k_ref.shape[0] // tk

    def scores(j, s_ref):
        off = pl.multiple_of(j * tk, tk)
        k = k_ref[pl.ds(off, tk), :]
        st = jnp.dot(k, qt, preferred_element_type=F32)
        s_ref[...] = st
        return jnp.max(st, axis=0, keepdims=True)

    acc_ref[...] = jnp.zeros(acc_ref.shape, F32)

    def half(j, m, l, mx, cur_ref, nxt_ref):
        mx_next = scores(jnp.minimum(j + 1, n_kv - 1), nxt_ref)
        off = pl.multiple_of(j * tk, tk)
        vt = vt_ref[:, pl.ds(off, tk)]
        m_new = jnp.maximum(m, mx)
        alpha = jnp.exp2(m - m_new)
        p = jnp.exp2(cur_ref[...] - m_new)
        l = alpha * l + jnp.sum(p, axis=0, keepdims=True)
        acc_ref[...] = alpha * acc_ref[...] + jnp.dot(vt, p.astype(vt.dtype), preferred_element_type=F32)
        return m_new, l, mx_next

    def step(jj, carry):
        m, l, mx = carry
        m, l, mx = half(2 * jj, m, l, mx, s0_ref, s1_ref)
        m, l, mx = half(2 * jj + 1, m, l, mx, s1_ref, s0_ref)
        return m, l, mx

    m0 = jnp.full((1, tq), MASK_VALUE, F32)
    l0 = jnp.zeros((1, tq), F32)
    _, l, _ = lax.fori_loop(0, n_kv // 2, step, (m0, l0, scores(0, s0_ref)))
    o_ref[...] = (acc_ref[...] / l).T.astype(o_ref.dtype)


def dense_attention(qt, k_nope, kr, vt, *, tq=512, tk=1024):
    h, dq, s = qt.shape
    dv = vt.shape[1]
    tq = min(tq, s)
    tk = min(tk, s // 2)
    assert (s // tk) % 2 == 0
    return pl.pallas_call(
        functools.partial(_flash_body, tk=tk),
        grid=(h, s // tq),
        in_specs=[
            pl.BlockSpec((None, dq, tq), lambda hh, i: (hh, 0, i)),
            pl.BlockSpec((s, QK_NOPE), lambda hh, i: (0, hh)),
            pl.BlockSpec((s, LANES), lambda hh, i: (0, 0)),
            pl.BlockSpec((None, dv, s), lambda hh, i: (hh, 0, 0)),
        ],
        out_specs=pl.BlockSpec((tq, dv), lambda hh, i: (i, hh)),
        out_shape=jax.ShapeDtypeStruct((s, h * dv), BF16),
        scratch_shapes=[
            pltpu.VMEM((s, dq), BF16),
            pltpu.VMEM((tk, tq), F32),
            pltpu.VMEM((tk, tq), F32),
            pltpu.VMEM((dv, tq), F32),
        ],
        compiler_params=_params("arbitrary", "arbitrary"),
        name="mla_flash",
    )(qt, k_nope, kr, vt)


def _dilated_steps(t):
    steps = []
    for g, (window, d) in enumerate(DILATED_GROUPS):
        n = -(-(window // 2) // t)
        steps += [(g, 0)] + [(g, o) for o in range(-n, n + 1) if o != 0]
    return steps


def _step_lookup(step, values):
    out = jnp.int32(values[-1])
    for idx in range(len(values) - 2, -1, -1):
        out = jnp.where(step <= idx, jnp.int32(values[idx]), out)
    return out


def _dilated_body(q_ref, k_ref, vt_ref, o_ref, m_sc, l_sc, acc_sc, s_sc, *, t, groups, offsets, nblk):
    i = pl.program_id(0)
    step = pl.program_id(1)
    nsteps = pl.num_programs(1)

    @pl.when(step == 0)
    def _():
        m_sc[...] = jnp.full(m_sc.shape, MASK_VALUE, F32)
        l_sc[...] = jnp.zeros(l_sc.shape, F32)
        acc_sc[...] = jnp.zeros(acc_sc.shape, F32)

    off = _step_lookup(step, offsets)
    dil = _step_lookup(step, [DILATED_GROUPS[gg][1] for gg in groups])
    half = _step_lookup(step, [DILATED_GROUPS[gg][0] // 2 for gg in groups])
    kb = i + off

    @pl.when(jnp.logical_and(kb >= 0, kb < nblk))
    def _():
        key = lax.broadcasted_iota(jnp.int32, (t, t), 0)
        qry = lax.broadcasted_iota(jnp.int32, (t, t), 1)
        dist = key - qry + off * t
        valid = (jnp.abs(dist) <= half) & ((dist & (dil - 1)) == 0)
        for hh in range(HEADS_PER_GROUP_A):
            sl = slice(hh * HEAD_DIM_A, (hh + 1) * HEAD_DIM_A)
            st = jnp.dot(k_ref[:, sl], q_ref[sl, :], preferred_element_type=F32)
            s_sc[hh] = jnp.where(valid, st, MASK_VALUE)
        for hh in range(HEADS_PER_GROUP_A):
            sl = slice(hh * HEAD_DIM_A, (hh + 1) * HEAD_DIM_A)
            st = s_sc[hh]
            m_prev = m_sc[hh]
            m_new = jnp.maximum(m_prev, jnp.max(st, axis=0, keepdims=True))
            alpha = jnp.exp2(m_prev - m_new)
            p = jnp.exp2(st - m_new)
            l_sc[hh] = alpha * l_sc[hh] + jnp.sum(p, axis=0, keepdims=True)
            vt = vt_ref[sl, :]
            acc_sc[sl, :] = alpha * acc_sc[sl, :] + jnp.dot(vt, p.astype(vt.dtype), preferred_element_type=F32)
            m_sc[hh] = m_new

    @pl.when(step == nsteps - 1)
    def _():
        for hh in range(HEADS_PER_GROUP_A):
            sl = slice(hh * HEAD_DIM_A, (hh + 1) * HEAD_DIM_A)
            o_ref[:, sl] = (acc_sc[sl, :] / l_sc[hh]).T.astype(o_ref.dtype)


def dilated_attention(qt, k, vt, *, t=256):
    s = k.shape[0]
    t = min(t, s)
    nblk = s // t
    steps = _dilated_steps(t)
    groups = [g for g, _ in steps]
    offsets = [o for _, o in steps]

    def q_map(i, st):
        return (_step_lookup(st, groups), i)

    def k_map(i, st):
        return (jnp.clip(i + _step_lookup(st, offsets), 0, nblk - 1), _step_lookup(st, groups))

    def v_map(i, st):
        return (_step_lookup(st, groups), jnp.clip(i + _step_lookup(st, offsets), 0, nblk - 1))

    return pl.pallas_call(
        functools.partial(_dilated_body, t=t, groups=groups, offsets=offsets, nblk=nblk),
        grid=(nblk, len(steps)),
        in_specs=[
            pl.BlockSpec((GROUP_WIDTH_A, t), q_map),
            pl.BlockSpec((t, GROUP_WIDTH_A), k_map),
            pl.BlockSpec((GROUP_WIDTH_A, t), v_map),
        ],
        out_specs=pl.BlockSpec((t, GROUP_WIDTH_A), lambda i, st: (i, 0)),
        out_shape=jax.ShapeDtypeStruct((s, GROUP_WIDTH_A), BF16),
        scratch_shapes=[
            pltpu.VMEM((HEADS_PER_GROUP_A, 1, t), F32),
            pltpu.VMEM((HEADS_PER_GROUP_A, 1, t), F32),
            pltpu.VMEM((GROUP_WIDTH_A, t), F32),
            pltpu.VMEM((HEADS_PER_GROUP_A, t, t), F32),
        ],
        compiler_params=_params("parallel", "arbitrary"),
        name="dilated_attn",
    )(qt, k, vt)


def _merge_body(oa_ref, ob_ref, h_ref, woa_ref, wob_ref, wga_ref, wgb_ref, ba_ref, bb_ref, o_ref):
    h = h_ref[...]

    def gate(wg_ref, b_ref):
        z = jnp.dot(h, wg_ref[...], preferred_element_type=F32) + b_ref[...]
        return 1.0 / (1.0 + jnp.exp(-z))

    ya = jnp.dot(oa_ref[...], woa_ref[...], preferred_element_type=F32)
    yb = jnp.dot(ob_ref[...], wob_ref[...], preferred_element_type=F32)
    o_ref[...] = (gate(wga_ref, ba_ref) * ya + gate(wgb_ref, bb_ref) * yb).astype(o_ref.dtype)


def gated_merge(oa, ob, h, w_oa, w_ob, w_g, b_g, *, tm=1024, tn=512):
    s = oa.shape[0]
    d = w_oa.shape[1]
    tm = min(tm, s)
    nj = d // tn
    return pl.pallas_call(
        _merge_body,
        grid=(s // tm, nj),
        in_specs=[
            pl.BlockSpec((tm, oa.shape[1]), lambda i, j: (i, 0)),
            pl.BlockSpec((tm, ob.shape[1]), lambda i, j: (i, 0)),
            pl.BlockSpec((tm, h.shape[1]), lambda i, j: (i, 0)),
            pl.BlockSpec((w_oa.shape[0], tn), lambda i, j: (0, j)),
            pl.BlockSpec((w_ob.shape[0], tn), lambda i, j: (0, j)),
            pl.BlockSpec((w_g.shape[0], tn), lambda i, j: (0, j)),
            pl.BlockSpec((w_g.shape[0], tn), lambda i, j: (0, j + nj)),
            pl.BlockSpec((1, tn), lambda i, j: (0, j)),
            pl.BlockSpec((1, tn), lambda i, j: (0, j + nj)),
        ],
        out_specs=pl.BlockSpec((tm, tn), lambda i, j: (i, j)),
        out_shape=jax.ShapeDtypeStruct((s, d), BF16),
        compiler_params=_params("parallel", "arbitrary"),
        name="gated_merge",
    )(oa, ob, h, w_oa, w_ob, w_g, w_g, b_g, b_g)


HALO = 16


def _ffn_body(x_ref, xp_ref, xn_ref, g_ref, wa_ref, wb_ref, cwa_ref, cwb_ref, cba_ref, cbb_ref, wd_ref,
              o_ref, h_sc):
    i = pl.program_id(0)
    j = pl.program_id(1)
    tm = x_ref.shape[0]

    def norm(x):
        ms = jnp.mean(x * x, axis=-1, keepdims=True)
        return x * lax.rsqrt(ms + EPS) * g_ref[...]

    @pl.when(j == 0)
    def _():
        x = x_ref[...]
        o_ref[...] = x
        h_sc[HALO:HALO + tm, :] = norm(x).astype(h_sc.dtype)
        hp = jnp.where(i > 0, norm(xp_ref[...]), 0.0)
        hn = jnp.where(i < pl.num_programs(0) - 1, norm(xn_ref[...]), 0.0)
        h_sc[:HALO, :] = hp.astype(h_sc.dtype)
        h_sc[HALO + tm:, :] = hn.astype(h_sc.dtype)

    h = h_sc[...]
    rows = tm + 2 * HALO

    def conv(w_ref, cw_ref, cb_ref):
        up = jnp.dot(h, w_ref[...], preferred_element_type=F32)
        cw = cw_ref[...]
        prev = pltpu.roll(up, 1, 0)
        nxt = pltpu.roll(up, rows - 1, 0)
        u = cb_ref[...] + prev * cw[0:1] + up * cw[1:2] + nxt * cw[2:3]
        return u[HALO:HALO + tm]

    ua = conv(wa_ref, cwa_ref, cba_ref)
    ub = conv(wb_ref, cwb_ref, cbb_ref)
    act = (ua / (1.0 + jnp.exp(-ua))) * ub
    o_ref[...] += jnp.dot(act.astype(wd_ref.dtype), wd_ref[...], preferred_element_type=F32)


def conv_ffn(x, g, w_up, conv_w, conv_b, w_down, *, tm=1024, tf=512):
    s, d = x.shape
    f = w_down.shape[0]
    tm = min(tm, s)
    nf = f // tf
    hb = tm // HALO
    n_halo_blocks = s // HALO
    return pl.pallas_call(
        _ffn_body,
        grid=(s // tm, nf),
        in_specs=[
            pl.BlockSpec((tm, d), lambda i, j: (i, 0)),
            pl.BlockSpec((HALO, d), lambda i, j: (jnp.maximum(i * hb - 1, 0), 0)),
            pl.BlockSpec((HALO, d), lambda i, j: (jnp.minimum((i + 1) * hb, n_halo_blocks - 1), 0)),
            pl.BlockSpec((1, d), lambda i, j: (0, 0)),
            pl.BlockSpec((d, tf), lambda i, j: (0, j)),
            pl.BlockSpec((d, tf), lambda i, j: (0, j + nf)),
            pl.BlockSpec((CONV_WIDTH, tf), lambda i, j: (0, j)),
            pl.BlockSpec((CONV_WIDTH, tf), lambda i, j: (0, j + nf)),
            pl.BlockSpec((1, tf), lambda i, j: (0, j)),
            pl.BlockSpec((1, tf), lambda i, j: (0, j + nf)),
            pl.BlockSpec((tf, d), lambda i, j: (j, 0)),
        ],
        out_specs=pl.BlockSpec((tm, d), lambda i, j: (i, 0)),
        out_shape=jax.ShapeDtypeStruct((s, d), F32),
        scratch_shapes=[pltpu.VMEM((tm + 2 * HALO, d), BF16)],
        compiler_params=_params("parallel", "arbitrary"),
        name="conv_ffn",
    )(x, x, x, g.reshape(1, d), w_up, w_up, conv_w, conv_w, conv_b.reshape(1, -1), conv_b.reshape(1, -1), w_down)


def _rope_tables(positions, dim):
    inv_freq = 1.0 / (ROPE_THETA ** (jnp.arange(0, dim, 2, dtype=F32) / dim))
    ang = positions.astype(F32)[:, None] * inv_freq
    cos = jnp.cos(ang)
    sin = jnp.sin(ang)
    return jnp.concatenate([cos, cos], axis=1), jnp.concatenate([-sin, sin], axis=1)


def kernel(x, positions, norm_mix, w_in, b_gate, norm_q, w_uq, norm_kv, w_ukv, w_oa, w_ob, w_out,
           norm_ffn, w_up, conv_w, conv_b, w_down, norm_final):
    batch, s, d = x.shape
    assert batch == 1
    depth = w_in.shape[0]
    xs = x.reshape(s, d)
    pos = positions.reshape(s)

    cos_a, sin_a = _rope_tables(pos, HEAD_DIM_A)
    scale_a = HEAD_DIM_A ** -0.5 * math.log2(math.e)
    cos_qk = jnp.stack([cos_a * scale_a, cos_a])
    sin_qk = jnp.stack([sin_a * scale_a, sin_a])
    cos_b, sin_b = _rope_tables(pos, QK_ROPE)
    pad = ((0, 0), (0, LANES - QK_ROPE))
    cos_bp, sin_bp = jnp.pad(cos_b, pad), jnp.pad(sin_b, pad)
    cos_bt = cos_b[:, QK_ROPE // 2:].T
    sin_bt = sin_b[:, QK_ROPE // 2:].T
    scale_b = QK_DIM_B ** -0.5 * math.log2(math.e)

    o_ql = 3 * WIDTH_A
    o_kvl = o_ql + Q_LORA
    o_kr = o_kvl + KV_LORA
    o_g = o_kr + QK_ROPE
    tn = 512

    for l in range(depth):
        w_in_l = w_in[l].astype(BF16)
        w_kr = jnp.pad(w_in_l[:, o_kr:o_g], ((0, 0), (0, LANES - QK_ROPE)))
        w_g = w_in_l[:, o_g:]

        tm = min(1024, s)
        qt_a, k_a, vt_a, lat, kr, h = in_proj(xs, norm_mix[l], w_in_l, w_kr, cos_qk, sin_qk, cos_bp, sin_bp, tm=tm)
        oa = dilated_attention(qt_a, k_a, vt_a)

        cq_t, ckv, ckv_t = latent_norm(lat, norm_q[l], norm_kv[l])
        w_uq_t = w_uq[l].T.astype(BF16)
        w_ukv_l = w_ukv[l].reshape(KV_LORA, N_HEADS_B, QK_NOPE + V_DIM)
        w_uk = w_ukv_l[:, :, :QK_NOPE].reshape(KV_LORA, N_HEADS_B * QK_NOPE).astype(BF16)
        w_uv_t = w_ukv_l[:, :, QK_NOPE:].reshape(KV_LORA, N_HEADS_B * V_DIM).T.astype(BF16)
        heads_per_tile = 4
        tq_rows = heads_per_tile * QK_DIM_B
        qt_b = matmul(
            w_uq_t, cq_t, functools.partial(_rope_rows_epilogue, scale=scale_b), (cos_bt, sin_bt),
            [pl.BlockSpec((QK_ROPE // 2, tm), lambda i, j: (0, j))] * 2,
            jax.ShapeDtypeStruct((N_HEADS_B * QK_DIM_B, s), BF16), pl.BlockSpec((tq_rows, tm), lambda i, j: (i, j)),
            tm=tq_rows, tn=tm, name="mla_q_proj").reshape(N_HEADS_B, QK_DIM_B, s)
        vt_b = matmul(
            w_uv_t, ckv_t, _store_epilogue, (), [],
            jax.ShapeDtypeStruct((N_HEADS_B * V_DIM, s), BF16), pl.BlockSpec((tm, tm), lambda i, j: (i, j)),
            tm=tm, tn=tm, name="mla_v_proj").reshape(N_HEADS_B, V_DIM, s)
        k_nope = matmul(
            ckv, w_uk, _store_epilogue, (), [],
            jax.ShapeDtypeStruct((s, N_HEADS_B * QK_NOPE), BF16), pl.BlockSpec((tm, tn), lambda i, j: (i, j)),
            tm=tm, tn=tn, name="mla_k_proj")
        ob = dense_attention(qt_b, k_nope, kr, vt_b)

        merged = gated_merge(oa, ob, h, w_oa[l].astype(BF16), w_ob[l].astype(BF16), w_g,
                             b_gate[l].reshape(1, -1))
        xs = matmul(
            merged, w_out[l].astype(BF16), _residual_epilogue, (xs,),
            [pl.BlockSpec((tm, tn), lambda i, j: (i, j))],
            jax.ShapeDtypeStruct((s, d), F32), pl.BlockSpec((tm, tn), lambda i, j: (i, j)),
            tm=tm, tn=tn, name="proj_out")

        xs = conv_ffn(xs, norm_ffn[l], w_up[l].astype(BF16), conv_w[l], conv_b[l], w_down[l].astype(BF16))

    out = rmsnorm(xs, norm_final, F32)
    return out.reshape(batch, s, d)
```

```python
import functools
import math

import jax
import jax.numpy as jnp
from jax import lax
from jax.experimental import pallas as pl
from jax.experimental.pallas import tpu as pltpu

F32 = jnp.float32
BF16 = jnp.bfloat16

HEAD_DIM_A = 128
HEADS_PER_GROUP_A = 4
DILATED_GROUPS = ((128, 1), (512, 4), (2048, 16))
GROUP_WIDTH_A = HEADS_PER_GROUP_A * HEAD_DIM_A
WIDTH_A = len(DILATED_GROUPS) * GROUP_WIDTH_A
N_HEADS_B = 16
QK_NOPE = 128
QK_ROPE = 64
V_DIM = 128
Q_LORA = 512
KV_LORA = 512
QK_DIM_B = QK_NOPE + QK_ROPE
ROPE_THETA = 10000.0
EPS = 1e-6
CONV_WIDTH = 3

LANES = 128
MASK_VALUE = -1e30
VMEM_LIMIT = 56 * 1024 * 1024


def _params(*sem):
    return pltpu.CompilerParams(dimension_semantics=sem, vmem_limit_bytes=VMEM_LIMIT)


def _rmsnorm_body(x_ref, g_ref, o_ref):
    x = x_ref[...].astype(F32)
    ms = jnp.mean(x * x, axis=-1, keepdims=True)
    o_ref[...] = (x * lax.rsqrt(ms + EPS) * g_ref[...]).astype(o_ref.dtype)


def rmsnorm(x, g, out_dtype, *, tm=512):
    s, width = x.shape
    tm = min(tm, s)
    return pl.pallas_call(
        _rmsnorm_body,
        grid=(s // tm,),
        in_specs=[
            pl.BlockSpec((tm, width), lambda i: (i, 0)),
            pl.BlockSpec((1, width), lambda i: (0, 0)),
        ],
        out_specs=pl.BlockSpec((tm, width), lambda i: (i, 0)),
        out_shape=jax.ShapeDtypeStruct((s, width), out_dtype),
        compiler_params=_params("parallel"),
        name="rmsnorm",
    )(x, g.reshape(1, width).astype(F32))


def _mm_body(a_ref, b_ref, *rest, epilogue, n_extra):
    acc = jnp.dot(a_ref[...], b_ref[...], preferred_element_type=F32)
    epilogue(acc, rest[:n_extra], rest[n_extra:])


def matmul(a, b, epilogue, extra, extra_specs, out_shapes, out_specs, *, tm, tn, name):
    m, k = a.shape
    n = b.shape[1]
    tm = min(tm, m)
    grid = (m // tm, n // tn)
    return pl.pallas_call(
        functools.partial(_mm_body, epilogue=epilogue, n_extra=len(extra)),
        grid=grid,
        in_specs=[
            pl.BlockSpec((tm, k), lambda i, j: (i, 0)),
            pl.BlockSpec((k, tn), lambda i, j: (0, j)),
            *extra_specs,
        ],
        out_specs=out_specs,
        out_shape=out_shapes,
        compiler_params=_params("parallel", "arbitrary"),
        name=name,
    )(a, b, *extra)


def _rope128_epilogue(acc, extra, outs):
    cos = extra[0][0]
    sin = extra[1][0]
    (o_ref,) = outs
    for c in range(acc.shape[1] // LANES):
        x = acc[:, c * LANES:(c + 1) * LANES]
        y = x * cos + pltpu.roll(x, LANES // 2, 1) * sin
        o_ref[:, c * LANES:(c + 1) * LANES] = y.astype(o_ref.dtype)


def _store_epilogue(acc, extra, outs):
    outs[0][...] = acc.astype(outs[0].dtype)


def _rope64_epilogue(acc, extra, outs):
    cos = extra[0][...]
    sin = extra[1][...]
    lane = lax.broadcasted_iota(jnp.int32, acc.shape, 1)
    half = QK_ROPE // 2
    rot = jnp.where(lane < half, pltpu.roll(acc, LANES - half, 1), pltpu.roll(acc, half, 1))
    outs[0][...] = (acc * cos + rot * sin).astype(outs[0].dtype)


def _residual_epilogue(acc, extra, outs):
    outs[0][...] = extra[0][...] + acc


IN_TN = 512
N_A_TILES = WIDTH_A // IN_TN
N_LAT_TILES = (Q_LORA + KV_LORA) // IN_TN


def _in_proj_body(x_ref, g_ref, w_ref, wkr_ref, cosa_ref, sina_ref, cosb_ref, sinb_ref,
                  qkv_ref, lat_ref, kr_ref, h_ref):
    j = pl.program_id(1)

    @pl.when(j == 0)
    def _():
        x = x_ref[...]
        ms = jnp.mean(x * x, axis=-1, keepdims=True)
        h = (x * lax.rsqrt(ms + EPS) * g_ref[...]).astype(h_ref.dtype)
        h_ref[...] = h
        _rope64_epilogue(jnp.dot(h, wkr_ref[...], preferred_element_type=F32), (cosb_ref, sinb_ref), (kr_ref,))

    acc = jnp.dot(h_ref[...], w_ref[...], preferred_element_type=F32)

    @pl.when(j < 2 * N_A_TILES)
    def _():
        _rope128_epilogue(acc, (cosa_ref, sina_ref), (qkv_ref,))

    @pl.when(jnp.logical_and(j >= 2 * N_A_TILES, j < 3 * N_A_TILES))
    def _():
        qkv_ref[...] = acc.astype(qkv_ref.dtype)

    @pl.when(j >= 3 * N_A_TILES)
    def _():
        lat_ref[...] = acc


def in_proj(x, g, w_in_l, w_kr, cos_qk, sin_qk, cos_bp, sin_bp, *, tm=1024):
    s, d = x.shape
    tm = min(tm, s)
    tn = IN_TN
    n_a = N_A_TILES
    table = pl.BlockSpec((1, tm, LANES), lambda i, j: (jnp.where(j < n_a, 0, 1), i, 0))
    return pl.pallas_call(
        _in_proj_body,
        grid=(s // tm, 3 * n_a + N_LAT_TILES),
        in_specs=[
            pl.BlockSpec((tm, d), lambda i, j: (i, 0)),
            pl.BlockSpec((1, d), lambda i, j: (0, 0)),
            pl.BlockSpec((d, tn), lambda i, j: (0, j)),
            pl.BlockSpec((d, LANES), lambda i, j: (0, 0)),
            table,
            table,
            pl.BlockSpec((tm, LANES), lambda i, j: (i, 0)),
            pl.BlockSpec((tm, LANES), lambda i, j: (i, 0)),
        ],
        out_specs=[
            pl.BlockSpec((tm, tn), lambda i, j: (i, jnp.minimum(j, 3 * n_a - 1))),
            pl.BlockSpec((tm, tn), lambda i, j: (i, jnp.clip(j - 3 * n_a, 0, N_LAT_TILES - 1))),
            pl.BlockSpec((tm, LANES), lambda i, j: (i, 0)),
            pl.BlockSpec((tm, d), lambda i, j: (i, 0)),
        ],
        out_shape=[
            jax.ShapeDtypeStruct((s, 3 * WIDTH_A), BF16),
            jax.ShapeDtypeStruct((s, Q_LORA + KV_LORA), F32),
            jax.ShapeDtypeStruct((s, LANES), BF16),
            jax.ShapeDtypeStruct((s, d), BF16),
        ],
        compiler_params=_params("arbitrary", "arbitrary"),
        name="in_proj",
    )(x, g.reshape(1, d), w_in_l, w_kr, cos_qk, sin_qk, cos_bp, sin_bp)


def _latent_norm_body(lat_ref, gq_ref, gkv_ref, cqt_ref, ckv_ref, ckvt_ref):
    def norm(x, g):
        ms = jnp.mean(x * x, axis=-1, keepdims=True)
        return x * lax.rsqrt(ms + EPS) * g

    cq = norm(lat_ref[:, :Q_LORA], gq_ref[...])
    ckv = norm(lat_ref[:, Q_LORA:], gkv_ref[...])
    cqt_ref[...] = cq.T.astype(cqt_ref.dtype)
    ckv_ref[...] = ckv.astype(ckv_ref.dtype)
    ckvt_ref[...] = ckv.T.astype(ckvt_ref.dtype)


def latent_norm(lat, g_q, g_kv, *, tm=512):
    s = lat.shape[0]
    tm = min(tm, s)
    return pl.pallas_call(
        _latent_norm_body,
        grid=(s // tm,),
        in_specs=[
            pl.BlockSpec((tm, Q_LORA + KV_LORA), lambda i: (i, 0)),
            pl.BlockSpec((1, Q_LORA), lambda i: (0, 0)),
            pl.BlockSpec((1, KV_LORA), lambda i: (0, 0)),
        ],
        out_specs=[
            pl.BlockSpec((Q_LORA, tm), lambda i: (0, i)),
            pl.BlockSpec((tm, KV_LORA), lambda i: (i, 0)),
            pl.BlockSpec((KV_LORA, tm), lambda i: (0, i)),
        ],
        out_shape=[
            jax.ShapeDtypeStruct((Q_LORA, s), BF16),
            jax.ShapeDtypeStruct((s, KV_LORA), BF16),
            jax.ShapeDtypeStruct((KV_LORA, s), BF16),
        ],
        compiler_params=_params("parallel"),
        name="latent_norm",
    )(lat, g_q.reshape(1, -1), g_kv.reshape(1, -1))


def _rope_rows_epilogue(acc, extra, outs, *, scale):
    cos = extra[0][...] * scale
    sin = extra[1][...] * scale
    (o_ref,) = outs
    half = QK_ROPE // 2
    for hh in range(acc.shape[0] // QK_DIM_B):
        r0 = hh * QK_DIM_B
        r1 = r0 + QK_NOPE
        x1 = acc[r1:r1 + half]
        x2 = acc[r1 + half:r0 + QK_DIM_B]
        o_ref[r0:r1, :] = (acc[r0:r1] * scale).astype(o_ref.dtype)
        o_ref[r1:r1 + half, :] = (x1 * cos - x2 * sin).astype(o_ref.dtype)
        o_ref[r1 + half:r0 + QK_DIM_B, :] = (x2 * cos + x1 * sin).astype(o_ref.dtype)


def _flash_body(q_ref, kn_ref, kr_ref, vt_ref, o_ref, k_ref, s0_ref, s1_ref, acc_ref, *, tk):
    @pl.when(pl.program_id(1) == 0)
    def _():
        k_ref[:, :QK_NOPE] = kn_ref[...]
        k_ref[:, QK_NOPE:] = kr_ref[:, :QK_ROPE]

    qt = q_ref[...]
    tq = qt.shape[1]
    n_kv = k_ref.shape[0] // tk

    def scores(j, s_ref):
        off = pl.multiple_of(j * tk, tk)
        k = k_ref[pl.ds(off, tk), :]
        st = jnp.dot(k, qt, preferred_element_type=F32)
        s_ref[...] = st
        return jnp.max(st, axis=0, keepdims=True)

    acc_ref[...] = jnp.zeros(acc_ref.shape, F32)

    def half(j, m, l, mx, cur_ref, nxt_ref):
        mx_next = scores(jnp.minimum(j + 1, n_kv - 1), nxt_ref)
        off = pl.multiple_of(j * tk, tk)
        vt = vt_ref[:, pl.ds(off, tk)]
        m_new = jnp.maximum(m, mx)
        alpha = jnp.exp2(m - m_new)
        p = jnp.exp2(cur_ref[...] - m_new)
        l = alpha * l + jnp.sum(p, axis=0, keepdims=True)
        acc_ref[...] = alpha * acc_ref[...] + jnp.dot(vt, p.astype(vt.dtype), preferred_element_type=F32)
        return m_new, l, mx_next

    def step(jj, carry):
        m, l, mx = carry
        m, l, mx = half(2 * jj, m, l, mx, s0_ref, s1_ref)
        m, l, mx = half(2 * jj + 1, m, l, mx, s1_ref, s0_ref)
        return m, l, mx

    m0 = jnp.full((1, tq), MASK_VALUE, F32)
    l0 = jnp.zeros((1, tq), F32)
    _, l, _ = lax.fori_loop(0, n_kv // 2, step, (m0, l0, scores(0, s0_ref)))
    o_ref[...] = (acc_ref[...] / l).T.astype(o_ref.dtype)


def dense_attention(qt, k_nope, kr, vt, *, tq=512, tk=1024):
    h, dq, s = qt.shape
    dv = vt.shape[1]
    tq = min(tq, s)
    tk = min(tk, s // 2)
    assert (s // tk) % 2 == 0
    return pl.pallas_call(
        functools.partial(_flash_body, tk=tk),
        grid=(h, s // tq),
        in_specs=[
            pl.BlockSpec((None, dq, tq), lambda hh, i: (hh, 0, i)),
            pl.BlockSpec((s, QK_NOPE), lambda hh, i: (0, hh)),
            pl.BlockSpec((s, LANES), lambda hh, i: (0, 0)),
            pl.BlockSpec((None, dv, s), lambda hh, i: (hh, 0, 0)),
        ],
        out_specs=pl.BlockSpec((tq, dv), lambda hh, i: (i, hh)),
        out_shape=jax.ShapeDtypeStruct((s, h * dv), BF16),
        scratch_shapes=[
            pltpu.VMEM((s, dq), BF16),
            pltpu.VMEM((tk, tq), F32),
            pltpu.VMEM((tk, tq), F32),
            pltpu.VMEM((dv, tq), F32),
        ],
        compiler_params=_params("arbitrary", "arbitrary"),
        name="mla_flash",
    )(qt, k_nope, kr, vt)


HALO_A = 64
assert all(window // (2 * dilation) == HALO_A for window, dilation in DILATED_GROUPS)


def _dilated_group_body(q_ref, kp_ref, km_ref, kn_ref, vp_ref, vm_ref, vn_ref, o_ref, lse_ref, k_sc, v_sc):
    b = pl.program_id(1)
    nb = pl.num_programs(1)
    t = q_ref.shape[0]
    w = t + 2 * HALO_A
    k_sc[:HALO_A] = kp_ref[...]
    k_sc[HALO_A:HALO_A + t] = km_ref[...]
    k_sc[HALO_A + t:] = kn_ref[...]
    v_sc[:HALO_A] = vp_ref[...]
    v_sc[HALO_A:HALO_A + t] = vm_ref[...]
    v_sc[HALO_A + t:] = vn_ref[...]
    qry = lax.broadcasted_iota(jnp.int32, (t, w), 0)
    key = lax.broadcasted_iota(jnp.int32, (t, w), 1)
    dist = key - HALO_A - qry
    lo = jnp.where(b > 0, 0, HALO_A)
    hi = jnp.where(b < nb - 1, w, HALO_A + t)
    valid = (jnp.abs(dist) <= HALO_A) & (key >= lo) & (key < hi)
    lane = lax.broadcasted_iota(jnp.int32, (t, LANES), 1)
    lse = jnp.zeros((t, LANES), F32)
    for hh in range(HEADS_PER_GROUP_A):
        sl = slice(hh * HEAD_DIM_A, (hh + 1) * HEAD_DIM_A)
        s = lax.dot_general(q_ref[:, sl], k_sc[:, sl], (((1,), (1,)), ((), ())), preferred_element_type=F32)
        s = jnp.where(valid, s, MASK_VALUE)
        m = jnp.max(s, axis=1, keepdims=True)
        p = jnp.exp2(s - m)
        l = jnp.sum(p, axis=1, keepdims=True)
        o = jnp.dot(p.astype(v_sc.dtype), v_sc[:, sl], preferred_element_type=F32)
        o_ref[:, sl] = o / l
        lse = jnp.where(lane == hh, m + jnp.log2(l), lse)
    lse_ref[...] = lse


def dilated_group(qkv, g, dilation, *, t=256):
    s, width = qkv.shape
    length = s // dilation
    t = min(t, length)
    n_sec = width // GROUP_WIDTH_A
    hb = t // HALO_A
    n_halo = length // HALO_A
    view = qkv.reshape(length, dilation * width)

    def col(sec):
        return lambda r: r * n_sec + sec * N_A_TILES + g

    def main(sec):
        c = col(sec)
        return pl.BlockSpec((t, GROUP_WIDTH_A), lambda r, b: (b, c(r)))

    def prev(sec):
        c = col(sec)
        return pl.BlockSpec((HALO_A, GROUP_WIDTH_A), lambda r, b: (jnp.maximum(b * hb - 1, 0), c(r)))

    def nxt(sec):
        c = col(sec)
        return pl.BlockSpec((HALO_A, GROUP_WIDTH_A), lambda r, b: (jnp.minimum((b + 1) * hb, n_halo - 1), c(r)))

    o, lse = pl.pallas_call(
        _dilated_group_body,
        grid=(dilation, length // t),
        in_specs=[main(0), prev(1), main(1), nxt(1), prev(2), main(2), nxt(2)],
        out_specs=[
            pl.BlockSpec((t, GROUP_WIDTH_A), lambda r, b: (b, r)),
            pl.BlockSpec((t, LANES), lambda r, b: (b, r)),
        ],
        out_shape=[
            jax.ShapeDtypeStruct((length, dilation * GROUP_WIDTH_A), F32),
            jax.ShapeDtypeStruct((length, dilation * LANES), F32),
        ],
        scratch_shapes=[
            pltpu.VMEM((t + 2 * HALO_A, GROUP_WIDTH_A), BF16),
            pltpu.VMEM((t + 2 * HALO_A, GROUP_WIDTH_A), BF16),
        ],
        compiler_params=_params("parallel", "parallel"),
        name="dilated_group",
    )(view, view, view, view, view, view, view)
    return o.reshape(s, GROUP_WIDTH_A), lse.reshape(s, LANES)


def _dilated_combine_body(*refs):
    n = len(DILATED_GROUPS)
    o_refs, l_refs, out_ref = refs[:n], refs[n:2 * n], refs[2 * n]
    for hh in range(HEADS_PER_GROUP_A):
        sl = slice(hh * HEAD_DIM_A, (hh + 1) * HEAD_DIM_A)
        ls = [l_ref[:, hh:hh + 1] for l_ref in l_refs]
        m = functools.reduce(jnp.maximum, ls)
        ws = [jnp.exp2(x - m) for x in ls]
        den = functools.reduce(jnp.add, ws)
        acc = functools.reduce(jnp.add, [wg * o_ref[:, sl] for wg, o_ref in zip(ws, o_refs)])
        out_ref[:, sl] = (acc / den).astype(out_ref.dtype)


def dilated_attention(qkv, *, tm=512):
    s = qkv.shape[0]
    tm = min(tm, s)
    outs, lses = zip(*[dilated_group(qkv, g, dilation) for g, (_, dilation) in enumerate(DILATED_GROUPS)])
    o_spec = pl.BlockSpec((tm, GROUP_WIDTH_A), lambda i: (i, 0))
    l_spec = pl.BlockSpec((tm, LANES), lambda i: (i, 0))
    return pl.pallas_call(
        _dilated_combine_body,
        grid=(s // tm,),
        in_specs=[o_spec] * len(outs) + [l_spec] * len(lses),
        out_specs=o_spec,
        out_shape=jax.ShapeDtypeStruct((s, GROUP_WIDTH_A), BF16),
        compiler_params=_params("parallel"),
        name="dilated_combine",
    )(*outs, *lses)


def _merge_body(oa_ref, ob_ref, h_ref, woa_ref, wob_ref, wga_ref, wgb_ref, ba_ref, bb_ref, o_ref):
    h = h_ref[...]

    def gate(wg_ref, b_ref):
        z = jnp.dot(h, wg_ref[...], preferred_element_type=F32) + b_ref[...]
        return 1.0 / (1.0 + jnp.exp(-z))

    ya = jnp.dot(oa_ref[...], woa_ref[...], preferred_element_type=F32)
    yb = jnp.dot(ob_ref[...], wob_ref[...], preferred_element_type=F32)
    o_ref[...] = (gate(wga_ref, ba_ref) * ya + gate(wgb_ref, bb_ref) * yb).astype(o_ref.dtype)


def gated_merge(oa, ob, h, w_oa, w_ob, w_g, b_g, *, tm=1024, tn=512):
    s = oa.shape[0]
    d = w_oa.shape[1]
    tm = min(tm, s)
    nj = d // tn
    return pl.pallas_call(
        _merge_body,
        grid=(s // tm, nj),
        in_specs=[
            pl.BlockSpec((tm, oa.shape[1]), lambda i, j: (i, 0)),
            pl.BlockSpec((tm, ob.shape[1]), lambda i, j: (i, 0)),
            pl.BlockSpec((tm, h.shape[1]), lambda i, j: (i, 0)),
            pl.BlockSpec((w_oa.shape[0], tn), lambda i, j: (0, j)),
            pl.BlockSpec((w_ob.shape[0], tn), lambda i, j: (0, j)),
            pl.BlockSpec((w_g.shape[0], tn), lambda i, j: (0, j)),
            pl.BlockSpec((w_g.shape[0], tn), lambda i, j: (0, j + nj)),
            pl.BlockSpec((1, tn), lambda i, j: (0, j)),
            pl.BlockSpec((1, tn), lambda i, j: (0, j + nj)),
        ],
        out_specs=pl.BlockSpec((tm, tn), lambda i, j: (i, j)),
        out_shape=jax.ShapeDtypeStruct((s, d), BF16),
        compiler_params=_params("parallel", "arbitrary"),
        name="gated_merge",
    )(oa, ob, h, w_oa, w_ob, w_g, w_g, b_g, b_g)


HALO = 16


def _ffn_body(x_ref, xp_ref, xn_ref, g_ref, wa_ref, wb_ref, cwa_ref, cwb_ref, cba_ref, cbb_ref, wd_ref,
              o_ref, h_sc):
    i = pl.program_id(0)
    j = pl.program_id(1)
    tm = x_ref.shape[0]

    def norm(x):
        ms = jnp.mean(x * x, axis=-1, keepdims=True)
        return x * lax.rsqrt(ms + EPS) * g_ref[...]

    @pl.when(j == 0)
    def _():
        x = x_ref[...]
        o_ref[...] = x
        h_sc[HALO:HALO + tm, :] = norm(x).astype(h_sc.dtype)
        hp = jnp.where(i > 0, norm(xp_ref[...]), 0.0)
        hn = jnp.where(i < pl.num_programs(0) - 1, norm(xn_ref[...]), 0.0)
        h_sc[:HALO, :] = hp.astype(h_sc.dtype)
        h_sc[HALO + tm:, :] = hn.astype(h_sc.dtype)

    h = h_sc[...]
    rows = tm + 2 * HALO

    def conv(w_ref, cw_ref, cb_ref):
        up = jnp.dot(h, w_ref[...], preferred_element_type=F32)
        cw = cw_ref[...]
        prev = pltpu.roll(up, 1, 0)
        nxt = pltpu.roll(up, rows - 1, 0)
        u = cb_ref[...] + prev * cw[0:1] + up * cw[1:2] + nxt * cw[2:3]
        return u[HALO:HALO + tm]

    ua = conv(wa_ref, cwa_ref, cba_ref)
    ub = conv(wb_ref, cwb_ref, cbb_ref)
    act = (ua / (1.0 + jnp.exp(-ua))) * ub
    o_ref[...] += jnp.dot(act.astype(wd_ref.dtype), wd_ref[...], preferred_element_type=F32)


def conv_ffn(x, g, w_up, conv_w, conv_b, w_down, *, tm=1024, tf=512):
    s, d = x.shape
    f = w_down.shape[0]
    tm = min(tm, s)
    nf = f // tf
    hb = tm // HALO
    n_halo_blocks = s // HALO
    return pl.pallas_call(
        _ffn_body,
        grid=(s // tm, nf),
        in_specs=[
            pl.BlockSpec((tm, d), lambda i, j: (i, 0)),
            pl.BlockSpec((HALO, d), lambda i, j: (jnp.maximum(i * hb - 1, 0), 0)),
            pl.BlockSpec((HALO, d), lambda i, j: (jnp.minimum((i + 1) * hb, n_halo_blocks - 1), 0)),
            pl.BlockSpec((1, d), lambda i, j: (0, 0)),
            pl.BlockSpec((d, tf), lambda i, j: (0, j)),
            pl.BlockSpec((d, tf), lambda i, j: (0, j + nf)),
            pl.BlockSpec((CONV_WIDTH, tf), lambda i, j: (0, j)),
            pl.BlockSpec((CONV_WIDTH, tf), lambda i, j: (0, j + nf)),
            pl.BlockSpec((1, tf), lambda i, j: (0, j)),
            pl.BlockSpec((1, tf), lambda i, j: (0, j + nf)),
            pl.BlockSpec((tf, d), lambda i, j: (j, 0)),
        ],
        out_specs=pl.BlockSpec((tm, d), lambda i, j: (i, 0)),
        out_shape=jax.ShapeDtypeStruct((s, d), F32),
        scratch_shapes=[pltpu.VMEM((tm + 2 * HALO, d), BF16)],
        compiler_params=_params("parallel", "arbitrary"),
        name="conv_ffn",
    )(x, x, x, g.reshape(1, d), w_up, w_up, conv_w, conv_w, conv_b.reshape(1, -1), conv_b.reshape(1, -1), w_down)


def _rope_tables(positions, dim):
    inv_freq = 1.0 / (ROPE_THETA ** (jnp.arange(0, dim, 2, dtype=F32) / dim))
    ang = positions.astype(F32)[:, None] * inv_freq
    cos = jnp.cos(ang)
    sin = jnp.sin(ang)
    return jnp.concatenate([cos, cos], axis=1), jnp.concatenate([-sin, sin], axis=1)


def kernel(x, positions, norm_mix, w_in, b_gate, norm_q, w_uq, norm_kv, w_ukv, w_oa, w_ob, w_out,
           norm_ffn, w_up, conv_w, conv_b, w_down, norm_final):
    batch, s, d = x.shape
    assert batch == 1
    depth = w_in.shape[0]
    xs = x.reshape(s, d)
    pos = positions.reshape(s)

    cos_a, sin_a = _rope_tables(pos, HEAD_DIM_A)
    scale_a = HEAD_DIM_A ** -0.5 * math.log2(math.e)
    cos_qk = jnp.stack([cos_a * scale_a, cos_a])
    sin_qk = jnp.stack([sin_a * scale_a, sin_a])
    cos_b, sin_b = _rope_tables(pos, QK_ROPE)
    pad = ((0, 0), (0, LANES - QK_ROPE))
    cos_bp, sin_bp = jnp.pad(cos_b, pad), jnp.pad(sin_b, pad)
    cos_bt = cos_b[:, QK_ROPE // 2:].T
    sin_bt = sin_b[:, QK_ROPE // 2:].T
    scale_b = QK_DIM_B ** -0.5 * math.log2(math.e)

    o_ql = 3 * WIDTH_A
    o_kvl = o_ql + Q_LORA
    o_kr = o_kvl + KV_LORA
    o_g = o_kr + QK_ROPE
    tn = 512

    for l in range(depth):
        w_in_l = w_in[l].astype(BF16)
        w_kr = jnp.pad(w_in_l[:, o_kr:o_g], ((0, 0), (0, LANES - QK_ROPE)))
        w_g = w_in_l[:, o_g:]

        tm = min(1024, s)
        qkv, lat, kr, h = in_proj(xs, norm_mix[l], w_in_l, w_kr, cos_qk, sin_qk, cos_bp, sin_bp, tm=tm)
        oa = dilated_attention(qkv)

        cq_t, ckv, ckv_t = latent_norm(lat, norm_q[l], norm_kv[l])
        w_uq_t = w_uq[l].T.astype(BF16)
        w_ukv_l = w_ukv[l].reshape(KV_LORA, N_HEADS_B, QK_NOPE + V_DIM)
        w_uk = w_ukv_l[:, :, :QK_NOPE].reshape(KV_LORA, N_HEADS_B * QK_NOPE).astype(BF16)
        w_uv_t = w_ukv_l[:, :, QK_NOPE:].reshape(KV_LORA, N_HEADS_B * V_DIM).T.astype(BF16)
        heads_per_tile = 4
        tq_rows = heads_per_tile * QK_DIM_B
        qt_b = matmul(
            w_uq_t, cq_t, functools.partial(_rope_rows_epilogue, scale=scale_b), (cos_bt, sin_bt),
            [pl.BlockSpec((QK_ROPE // 2, tm), lambda i, j: (0, j))] * 2,
            jax.ShapeDtypeStruct((N_HEADS_B * QK_DIM_B, s), BF16), pl.BlockSpec((tq_rows, tm), lambda i, j: (i, j)),
            tm=tq_rows, tn=tm, name="mla_q_proj").reshape(N_HEADS_B, QK_DIM_B, s)
        vt_b = matmul(
            w_uv_t, ckv_t, _store_epilogue, (), [],
            jax.ShapeDtypeStruct((N_HEADS_B * V_DIM, s), BF16), pl.BlockSpec((tm, tm), lambda i, j: (i, j)),
            tm=tm, tn=tm, name="mla_v_proj").reshape(N_HEADS_B, V_DIM, s)
        k_nope = matmul(
            ckv, w_uk, _store_epilogue, (), [],
            jax.ShapeDtypeStruct((s, N_HEADS_B * QK_NOPE), BF16), pl.BlockSpec((tm, tn), lambda i, j: (i, j)),
            tm=tm, tn=tn, name="mla_k_proj")
        ob = dense_attention(qt_b, k_nope, kr, vt_b)

        merged = gated_merge(oa, ob, h, w_oa[l].astype(BF16), w_ob[l].astype(BF16), w_g,
                             b_gate[l].reshape(1, -1))
        xs = matmul(
            merged, w_out[l].astype(BF16), _residual_epilogue, (xs,),
            [pl.BlockSpec((tm, tn), lambda i, j: (i, j))],
            jax.ShapeDtypeStruct((s, d), F32), pl.BlockSpec((tm, tn), lambda i, j: (i, j)),
            tm=tm, tn=tn, name="proj_out")

        xs = conv_ffn(xs, norm_ffn[l], w_up[l].astype(BF16), conv_w[l], conv_b[l], w_down[l].astype(BF16))

    out = rmsnorm(xs, norm_final, F32)
    return out.reshape(batch, s, d)
```

```python
import functools
import math

import jax
import jax.numpy as jnp
from jax import lax
from jax.experimental import pallas as pl
from jax.experimental.pallas import tpu as pltpu

F32 = jnp.float32
BF16 = jnp.bfloat16

HEAD_DIM_A = 128
HEADS_PER_GROUP_A = 4
DILATED_GROUPS = ((128, 1), (512, 4), (2048, 16))
GROUP_WIDTH_A = HEADS_PER_GROUP_A * HEAD_DIM_A
WIDTH_A = len(DILATED_GROUPS) * GROUP_WIDTH_A
N_HEADS_B = 16
QK_NOPE = 128
QK_ROPE = 64
V_DIM = 128
Q_LORA = 512
KV_LORA = 512
QK_DIM_B = QK_NOPE + QK_ROPE
ROPE_THETA = 10000.0
EPS = 1e-6
CONV_WIDTH = 3

LANES = 128
MASK_VALUE = -1e30
VMEM_LIMIT = 56 * 1024 * 1024


def _params(*sem):
    return pltpu.CompilerParams(dimension_semantics=sem, vmem_limit_bytes=VMEM_LIMIT)


def _rmsnorm_body(x_ref, g_ref, o_ref):
    x = x_ref[...].astype(F32)
    ms = jnp.mean(x * x, axis=-1, keepdims=True)
    o_ref[...] = (x * lax.rsqrt(ms + EPS) * g_ref[...]).astype(o_ref.dtype)


def rmsnorm(x, g, out_dtype, *, tm=512):
    s, width = x.shape
    tm = min(tm, s)
    return pl.pallas_call(
        _rmsnorm_body,
        grid=(s // tm,),
        in_specs=[
            pl.BlockSpec((tm, width), lambda i: (i, 0)),
            pl.BlockSpec((1, width), lambda i: (0, 0)),
        ],
        out_specs=pl.BlockSpec((tm, width), lambda i: (i, 0)),
        out_shape=jax.ShapeDtypeStruct((s, width), out_dtype),
        compiler_params=_params("parallel"),
        name="rmsnorm",
    )(x, g.reshape(1, width).astype(F32))


def _mm_body(a_ref, b_ref, *rest, epilogue, n_extra):
    acc = jnp.dot(a_ref[...], b_ref[...], preferred_element_type=F32)
    epilogue(acc, rest[:n_extra], rest[n_extra:])


def matmul(a, b, epilogue, extra, extra_specs, out_shapes, out_specs, *, tm, tn, name):
    m, k = a.shape
    n = b.shape[1]
    tm = min(tm, m)
    grid = (m // tm, n // tn)
    return pl.pallas_call(
        functools.partial(_mm_body, epilogue=epilogue, n_extra=len(extra)),
        grid=grid,
        in_specs=[
            pl.BlockSpec((tm, k), lambda i, j: (i, 0)),
            pl.BlockSpec((k, tn), lambda i, j: (0, j)),
            *extra_specs,
        ],
        out_specs=out_specs,
        out_shape=out_shapes,
        compiler_params=_params("parallel", "arbitrary"),
        name=name,
    )(a, b, *extra)


def _rope128_epilogue(acc, extra, outs):
    cos = extra[0][0]
    sin = extra[1][0]
    (o_ref,) = outs
    for c in range(acc.shape[1] // LANES):
        x = acc[:, c * LANES:(c + 1) * LANES]
        y = x * cos + pltpu.roll(x, LANES // 2, 1) * sin
        o_ref[:, c * LANES:(c + 1) * LANES] = y.astype(o_ref.dtype)


def _store_epilogue(acc, extra, outs):
    outs[0][...] = acc.astype(outs[0].dtype)


def _rope64_epilogue(acc, extra, outs):
    cos = extra[0][...]
    sin = extra[1][...]
    lane = lax.broadcasted_iota(jnp.int32, acc.shape, 1)
    half = QK_ROPE // 2
    rot = jnp.where(lane < half, pltpu.roll(acc, LANES - half, 1), pltpu.roll(acc, half, 1))
    outs[0][...] = (acc * cos + rot * sin).astype(outs[0].dtype)


def _residual_epilogue(acc, extra, outs):
    outs[0][...] = extra[0][...] + acc


IN_TN = GROUP_WIDTH_A
N_SEC = 3
N_GROUPS_A = len(DILATED_GROUPS)
N_A_TILES = N_GROUPS_A * N_SEC
N_LAT_TILES = (Q_LORA + KV_LORA) // IN_TN
CHUNKS = IN_TN // LANES


def _in_proj_body(x_ref, g_ref, w_ref, wkr_ref, cosa_ref, sina_ref, cosb_ref, sinb_ref, *rest):
    a_refs = rest[:N_GROUPS_A]
    lat_ref, kr_ref, h_ref, y_sc = rest[N_GROUPS_A:]
    j = pl.program_id(1)
    tm = x_ref.shape[0]

    @pl.when(j == 0)
    def _():
        x = x_ref[...]
        ms = jnp.mean(x * x, axis=-1, keepdims=True)
        h = (x * lax.rsqrt(ms + EPS) * g_ref[...]).astype(h_ref.dtype)
        h_ref[...] = h
        _rope64_epilogue(jnp.dot(h, wkr_ref[...], preferred_element_type=F32), (cosb_ref, sinb_ref), (kr_ref,))

    acc = jnp.dot(h_ref[...], w_ref[...], preferred_element_type=F32)
    sec = j % N_SEC
    is_a = j < N_A_TILES

    @pl.when(jnp.logical_and(is_a, sec < 2))
    def _():
        cos = cosa_ref[0]
        sin = sina_ref[0]
        for c in range(CHUNKS):
            x = acc[:, c * LANES:(c + 1) * LANES]
            y_sc[c] = x * cos + pltpu.roll(x, LANES // 2, 1) * sin

    @pl.when(jnp.logical_and(is_a, sec == 2))
    def _():
        for c in range(CHUNKS):
            y_sc[c] = acc[:, c * LANES:(c + 1) * LANES]

    for gi, (_, dilation) in enumerate(DILATED_GROUPS):
        @pl.when(jnp.logical_and(j >= gi * N_SEC, j < (gi + 1) * N_SEC))
        def _(a_ref=a_refs[gi], dilation=dilation):
            n = tm // dilation
            for r in range(dilation):
                for c in range(CHUNKS):
                    rows = y_sc[c, pl.ds(r, n, stride=dilation), :] if dilation > 1 else y_sc[c]
                    a_ref[r, :, c * LANES:(c + 1) * LANES] = rows.astype(a_ref.dtype)

    @pl.when(j >= N_A_TILES)
    def _():
        lat_ref[...] = acc


def in_proj(x, g, w_cat, w_kr, cos_qk, sin_qk, cos_bp, sin_bp, *, tm=1024):
    s, d = x.shape
    tm = min(tm, s)
    tn = IN_TN
    table = pl.BlockSpec((1, tm, LANES), lambda i, j: (jnp.where(j % N_SEC == 0, 0, 1), i, 0))

    def a_spec(gi, dilation):
        return pl.BlockSpec((dilation, tm // dilation, tn),
                            lambda i, j: (0, i, jnp.clip(j - gi * N_SEC, 0, N_SEC - 1)))

    return pl.pallas_call(
        _in_proj_body,
        grid=(s // tm, N_A_TILES + N_LAT_TILES),
        in_specs=[
            pl.BlockSpec((tm, d), lambda i, j: (i, 0)),
            pl.BlockSpec((1, d), lambda i, j: (0, 0)),
            pl.BlockSpec((d, tn), lambda i, j: (0, j)),
            pl.BlockSpec((d, LANES), lambda i, j: (0, 0)),
            table,
            table,
            pl.BlockSpec((tm, LANES), lambda i, j: (i, 0)),
            pl.BlockSpec((tm, LANES), lambda i, j: (i, 0)),
        ],
        out_specs=[
            *[a_spec(gi, dilation) for gi, (_, dilation) in enumerate(DILATED_GROUPS)],
            pl.BlockSpec((tm, tn), lambda i, j: (i, jnp.clip(j - N_A_TILES, 0, N_LAT_TILES - 1))),
            pl.BlockSpec((tm, LANES), lambda i, j: (i, 0)),
            pl.BlockSpec((tm, d), lambda i, j: (i, 0)),
        ],
        out_shape=[
            *[jax.ShapeDtypeStruct((dilation, s // dilation, N_SEC * tn), BF16) for _, dilation in DILATED_GROUPS],
            jax.ShapeDtypeStruct((s, Q_LORA + KV_LORA), F32),
            jax.ShapeDtypeStruct((s, LANES), BF16),
            jax.ShapeDtypeStruct((s, d), BF16),
        ],
        scratch_shapes=[pltpu.VMEM((CHUNKS, tm, LANES), F32)],
        compiler_params=_params("arbitrary", "arbitrary"),
        name="in_proj",
    )(x, g.reshape(1, d), w_cat, w_kr, cos_qk, sin_qk, cos_bp, sin_bp)


def _latent_norm_body(lat_ref, gq_ref, gkv_ref, cqt_ref, ckv_ref, ckvt_ref):
    def norm(x, g):
        ms = jnp.mean(x * x, axis=-1, keepdims=True)
        return x * lax.rsqrt(ms + EPS) * g

    cq = norm(lat_ref[:, :Q_LORA], gq_ref[...])
    ckv = norm(lat_ref[:, Q_LORA:], gkv_ref[...])
    cqt_ref[...] = cq.T.astype(cqt_ref.dtype)
    ckv_ref[...] = ckv.astype(ckv_ref.dtype)
    ckvt_ref[...] = ckv.T.astype(ckvt_ref.dtype)


def latent_norm(lat, g_q, g_kv, *, tm=512):
    s = lat.shape[0]
    tm = min(tm, s)
    return pl.pallas_call(
        _latent_norm_body,
        grid=(s // tm,),
        in_specs=[
            pl.BlockSpec((tm, Q_LORA + KV_LORA), lambda i: (i, 0)),
            pl.BlockSpec((1, Q_LORA), lambda i: (0, 0)),
            pl.BlockSpec((1, KV_LORA), lambda i: (0, 0)),
        ],
        out_specs=[
            pl.BlockSpec((Q_LORA, tm), lambda i: (0, i)),
            pl.BlockSpec((tm, KV_LORA), lambda i: (i, 0)),
            pl.BlockSpec((KV_LORA, tm), lambda i: (0, i)),
        ],
        out_shape=[
            jax.ShapeDtypeStruct((Q_LORA, s), BF16),
            jax.ShapeDtypeStruct((s, KV_LORA), BF16),
            jax.ShapeDtypeStruct((KV_LORA, s), BF16),
        ],
        compiler_params=_params("parallel"),
        name="latent_norm",
    )(lat, g_q.reshape(1, -1), g_kv.reshape(1, -1))


def _rope_rows_epilogue(acc, extra, outs, *, scale):
    cos = extra[0][...] * scale
    sin = extra[1][...] * scale
    (o_ref,) = outs
    half = QK_ROPE // 2
    for hh in range(acc.shape[0] // QK_DIM_B):
        r0 = hh * QK_DIM_B
        r1 = r0 + QK_NOPE
        x1 = acc[r1:r1 + half]
        x2 = acc[r1 + half:r0 + QK_DIM_B]
        o_ref[r0:r1, :] = (acc[r0:r1] * scale).astype(o_ref.dtype)
        o_ref[r1:r1 + half, :] = (x1 * cos - x2 * sin).astype(o_ref.dtype)
        o_ref[r1 + half:r0 + QK_DIM_B, :] = (x2 * cos + x1 * sin).astype(o_ref.dtype)


def _flash_body(q_ref, kn_ref, kr_ref, vt_ref, o_ref, k_ref, s0_ref, s1_ref, acc_ref, *, tk):
    @pl.when(pl.program_id(1) == 0)
    def _():
        k_ref[:, :QK_NOPE] = kn_ref[...]
        k_ref[:, QK_NOPE:] = kr_ref[:, :QK_ROPE]

    qt = q_ref[...]
    tq = qt.shape[1]
    n_kv = k_ref.shape[0] // tk

    def scores(j, s_ref):
        off = pl.multiple_of(j * tk, tk)
        k = k_ref[pl.ds(off, tk), :]
        st = jnp.dot(k, qt, preferred_element_type=F32)
        s_ref[...] = st
        return jnp.max(st, axis=0, keepdims=True)

    acc_ref[...] = jnp.zeros(acc_ref.shape, F32)

    def half(j, m, l, mx, cur_ref, nxt_ref):
        mx_next = scores(jnp.minimum(j + 1, n_kv - 1), nxt_ref)
        off = pl.multiple_of(j * tk, tk)
        vt = vt_ref[:, pl.ds(off, tk)]
        m_new = jnp.maximum(m, mx)
        alpha = jnp.exp2(m - m_new)
        p = jnp.exp2(cur_ref[...] - m_new)
        l = alpha * l + jnp.sum(p, axis=0, keepdims=True)
        acc_ref[...] = alpha * acc_ref[...] + jnp.dot(vt, p.astype(vt.dtype), preferred_element_type=F32)
        return m_new, l, mx_next

    def step(jj, carry):
        m, l, mx = carry
        m, l, mx = half(2 * jj, m, l, mx, s0_ref, s1_ref)
        m, l, mx = half(2 * jj + 1, m, l, mx, s1_ref, s0_ref)
        return m, l, mx

    m0 = jnp.full((1, tq), MASK_VALUE, F32)
    l0 = jnp.zeros((1, tq), F32)
    _, l, _ = lax.fori_loop(0, n_kv // 2, step, (m0, l0, scores(0, s0_ref)))
    o_ref[...] = (acc_ref[...] / l).T.astype(o_ref.dtype)


def dense_attention(qt, k_nope, kr, vt, *, tq=512, tk=1024):
    h, dq, s = qt.shape
    dv = vt.shape[1]
    tq = min(tq, s)
    tk = min(tk, s // 2)
    assert (s // tk) % 2 == 0
    return pl.pallas_call(
        functools.partial(_flash_body, tk=tk),
        grid=(h, s // tq),
        in_specs=[
            pl.BlockSpec((None, dq, tq), lambda hh, i: (hh, 0, i)),
            pl.BlockSpec((s, QK_NOPE), lambda hh, i: (0, hh)),
            pl.BlockSpec((s, LANES), lambda hh, i: (0, 0)),
            pl.BlockSpec((None, dv, s), lambda hh, i: (hh, 0, 0)),
        ],
        out_specs=pl.BlockSpec((tq, dv), lambda hh, i: (i, hh)),
        out_shape=jax.ShapeDtypeStruct((s, h * dv), BF16),
        scratch_shapes=[
            pltpu.VMEM((s, dq), BF16),
            pltpu.VMEM((tk, tq), F32),
            pltpu.VMEM((tk, tq), F32),
            pltpu.VMEM((dv, tq), F32),
        ],
        compiler_params=_params("arbitrary", "arbitrary"),
        name="mla_flash",
    )(qt, k_nope, kr, vt)


HALO_A = 64
assert all(window // (2 * dilation) == HALO_A for window, dilation in DILATED_GROUPS)


def _dilated_group_body(q_ref, kp_ref, km_ref, kn_ref, vp_ref, vm_ref, vn_ref, o_ref, lse_ref, k_sc, v_sc):
    b = pl.program_id(1)
    nb = pl.num_programs(1)
    t = q_ref.shape[0]
    w = t + 2 * HALO_A
    k_sc[:HALO_A] = kp_ref[...]
    k_sc[HALO_A:HALO_A + t] = km_ref[...]
    k_sc[HALO_A + t:] = kn_ref[...]
    v_sc[:HALO_A] = vp_ref[...]
    v_sc[HALO_A:HALO_A + t] = vm_ref[...]
    v_sc[HALO_A + t:] = vn_ref[...]
    qry = lax.broadcasted_iota(jnp.int32, (t, w), 0)
    key = lax.broadcasted_iota(jnp.int32, (t, w), 1)
    dist = key - HALO_A - qry
    lo = jnp.where(b > 0, 0, HALO_A)
    hi = jnp.where(b < nb - 1, w, HALO_A + t)
    valid = (jnp.abs(dist) <= HALO_A) & (key >= lo) & (key < hi)
    lane = lax.broadcasted_iota(jnp.int32, (t, LANES), 1)
    lse = jnp.zeros((t, LANES), F32)
    for hh in range(HEADS_PER_GROUP_A):
        sl = slice(hh * HEAD_DIM_A, (hh + 1) * HEAD_DIM_A)
        s = lax.dot_general(q_ref[:, sl], k_sc[:, sl], (((1,), (1,)), ((), ())), preferred_element_type=F32)
        s = jnp.where(valid, s, MASK_VALUE)
        m = jnp.max(s, axis=1, keepdims=True)
        p = jnp.exp2(s - m)
        l = jnp.sum(p, axis=1, keepdims=True)
        o = jnp.dot(p.astype(v_sc.dtype), v_sc[:, sl], preferred_element_type=F32)
        o_ref[:, sl] = o / l
        lse = jnp.where(lane == hh, m + jnp.log2(l), lse)
    lse_ref[...] = lse


def dilated_group(a, *, t=256):
    dilation, length, _ = a.shape
    t = min(t, length)
    hb = t // HALO_A
    n_halo = length // HALO_A

    def main(sec):
        return pl.BlockSpec((None, t, GROUP_WIDTH_A), lambda r, b: (r, b, sec))

    def prev(sec):
        return pl.BlockSpec((None, HALO_A, GROUP_WIDTH_A), lambda r, b: (r, jnp.maximum(b * hb - 1, 0), sec))

    def nxt(sec):
        return pl.BlockSpec((None, HALO_A, GROUP_WIDTH_A),
                            lambda r, b: (r, jnp.minimum((b + 1) * hb, n_halo - 1), sec))

    return pl.pallas_call(
        _dilated_group_body,
        grid=(dilation, length // t),
        in_specs=[main(0), prev(1), main(1), nxt(1), prev(2), main(2), nxt(2)],
        out_specs=[
            pl.BlockSpec((None, t, GROUP_WIDTH_A), lambda r, b: (r, b, 0)),
            pl.BlockSpec((None, t, LANES), lambda r, b: (r, b, 0)),
        ],
        out_shape=[
            jax.ShapeDtypeStruct((dilation, length, GROUP_WIDTH_A), F32),
            jax.ShapeDtypeStruct((dilation, length, LANES), F32),
        ],
        scratch_shapes=[
            pltpu.VMEM((t + 2 * HALO_A, GROUP_WIDTH_A), BF16),
            pltpu.VMEM((t + 2 * HALO_A, GROUP_WIDTH_A), BF16),
        ],
        compiler_params=_params("parallel", "parallel"),
        name="dilated_group",
    )(a, a, a, a, a, a, a)


def _dilated_combine_body(*refs):
    n = N_GROUPS_A
    o_refs, l_refs = refs[:n], refs[n:2 * n]
    out_ref, o_sc, l_sc = refs[2 * n:]
    tm = out_ref.shape[0]
    for gi, (_, dilation) in enumerate(DILATED_GROUPS):
        rows = tm // dilation
        for r in range(dilation):
            dst = pl.ds(r, rows, stride=dilation) if dilation > 1 else slice(None)
            l_sc[gi, dst, :] = l_refs[gi][r]
            for hh in range(HEADS_PER_GROUP_A):
                o_sc[gi, hh, dst, :] = o_refs[gi][r, :, hh * HEAD_DIM_A:(hh + 1) * HEAD_DIM_A]
    for hh in range(HEADS_PER_GROUP_A):
        ls = [l_sc[gi, :, hh:hh + 1] for gi in range(n)]
        m = functools.reduce(jnp.maximum, ls)
        ws = [jnp.exp2(x - m) for x in ls]
        den = functools.reduce(jnp.add, ws)
        acc = functools.reduce(jnp.add, [ws[gi] * o_sc[gi, hh] for gi in range(n)])
        out_ref[:, hh * HEAD_DIM_A:(hh + 1) * HEAD_DIM_A] = (acc / den).astype(out_ref.dtype)


def dilated_attention(a_groups, *, tm=1024):
    s = a_groups[0].shape[0] * a_groups[0].shape[1]
    tm = min(tm, s)
    outs, lses = zip(*[dilated_group(a) for a in a_groups])

    def spec(dilation, width):
        return pl.BlockSpec((dilation, tm // dilation, width), lambda i: (0, i, 0))

    return pl.pallas_call(
        _dilated_combine_body,
        grid=(s // tm,),
        in_specs=[spec(d, GROUP_WIDTH_A) for _, d in DILATED_GROUPS] + [spec(d, LANES) for _, d in DILATED_GROUPS],
        out_specs=pl.BlockSpec((tm, GROUP_WIDTH_A), lambda i: (i, 0)),
        out_shape=jax.ShapeDtypeStruct((s, GROUP_WIDTH_A), BF16),
        scratch_shapes=[
            pltpu.VMEM((N_GROUPS_A, HEADS_PER_GROUP_A, tm, HEAD_DIM_A), F32),
            pltpu.VMEM((N_GROUPS_A, tm, LANES), F32),
        ],
        compiler_params=_params("parallel"),
        name="dilated_combine",
    )(*outs, *lses)


def _merge_body(oa_ref, ob_ref, h_ref, woa_ref, wob_ref, wga_ref, wgb_ref, ba_ref, bb_ref, o_ref):
    h = h_ref[...]

    def gate(wg_ref, b_ref):
        z = jnp.dot(h, wg_ref[...], preferred_element_type=F32) + b_ref[...]
        return 1.0 / (1.0 + jnp.exp(-z))

    ya = jnp.dot(oa_ref[...], woa_ref[...], preferred_element_type=F32)
    yb = jnp.dot(ob_ref[...], wob_ref[...], preferred_element_type=F32)
    o_ref[...] = (gate(wga_ref, ba_ref) * ya + gate(wgb_ref, bb_ref) * yb).astype(o_ref.dtype)


def gated_merge(oa, ob, h, w_oa, w_ob, w_g, b_g, *, tm=1024, tn=512):
    s = oa.shape[0]
    d = w_oa.shape[1]
    tm = min(tm, s)
    nj = d // tn
    return pl.pallas_call(
        _merge_body,
        grid=(s // tm, nj),
        in_specs=[
            pl.BlockSpec((tm, oa.shape[1]), lambda i, j: (i, 0)),
            pl.BlockSpec((tm, ob.shape[1]), lambda i, j: (i, 0)),
            pl.BlockSpec((tm, h.shape[1]), lambda i, j: (i, 0)),
            pl.BlockSpec((w_oa.shape[0], tn), lambda i, j: (0, j)),
            pl.BlockSpec((w_ob.shape[0], tn), lambda i, j: (0, j)),
            pl.BlockSpec((w_g.shape[0], tn), lambda i, j: (0, j)),
            pl.BlockSpec((w_g.shape[0], tn), lambda i, j: (0, j + nj)),
            pl.BlockSpec((1, tn), lambda i, j: (0, j)),
            pl.BlockSpec((1, tn), lambda i, j: (0, j + nj)),
        ],
        out_specs=pl.BlockSpec((tm, tn), lambda i, j: (i, j)),
        out_shape=jax.ShapeDtypeStruct((s, d), BF16),
        compiler_params=_params("parallel", "arbitrary"),
        name="gated_merge",
    )(oa, ob, h, w_oa, w_ob, w_g, w_g, b_g, b_g)


HALO = 16


def _ffn_body(x_ref, xp_ref, xn_ref, g_ref, wa_ref, wb_ref, cwa_ref, cwb_ref, cba_ref, cbb_ref, wd_ref,
              o_ref, h_sc):
    i = pl.program_id(0)
    j = pl.program_id(1)
    tm = x_ref.shape[0]

    def norm(x):
        ms = jnp.mean(x * x, axis=-1, keepdims=True)
        return x * lax.rsqrt(ms + EPS) * g_ref[...]

    @pl.when(j == 0)
    def _():
        x = x_ref[...]
        o_ref[...] = x
        h_sc[HALO:HALO + tm, :] = norm(x).astype(h_sc.dtype)
        hp = jnp.where(i > 0, norm(xp_ref[...]), 0.0)
        hn = jnp.where(i < pl.num_programs(0) - 1, norm(xn_ref[...]), 0.0)
        h_sc[:HALO, :] = hp.astype(h_sc.dtype)
        h_sc[HALO + tm:, :] = hn.astype(h_sc.dtype)

    h = h_sc[...]
    rows = tm + 2 * HALO

    def conv(w_ref, cw_ref, cb_ref):
        up = jnp.dot(h, w_ref[...], preferred_element_type=F32)
        cw = cw_ref[...]
        prev = pltpu.roll(up, 1, 0)
        nxt = pltpu.roll(up, rows - 1, 0)
        u = cb_ref[...] + prev * cw[0:1] + up * cw[1:2] + nxt * cw[2:3]
        return u[HALO:HALO + tm]

    ua = conv(wa_ref, cwa_ref, cba_ref)
    ub = conv(wb_ref, cwb_ref, cbb_ref)
    act = (ua / (1.0 + jnp.exp(-ua))) * ub
    o_ref[...] += jnp.dot(act.astype(wd_ref.dtype), wd_ref[...], preferred_element_type=F32)


def conv_ffn(x, g, w_up, conv_w, conv_b, w_down, *, tm=1024, tf=512):
    s, d = x.shape
    f = w_down.shape[0]
    tm = min(tm, s)
    nf = f // tf
    hb = tm // HALO
    n_halo_blocks = s // HALO
    return pl.pallas_call(
        _ffn_body,
        grid=(s // tm, nf),
        in_specs=[
            pl.BlockSpec((tm, d), lambda i, j: (i, 0)),
            pl.BlockSpec((HALO, d), lambda i, j: (jnp.maximum(i * hb - 1, 0), 0)),
            pl.BlockSpec((HALO, d), lambda i, j: (jnp.minimum((i + 1) * hb, n_halo_blocks - 1), 0)),
            pl.BlockSpec((1, d), lambda i, j: (0, 0)),
            pl.BlockSpec((d, tf), lambda i, j: (0, j)),
            pl.BlockSpec((d, tf), lambda i, j: (0, j + nf)),
            pl.BlockSpec((CONV_WIDTH, tf), lambda i, j: (0, j)),
            pl.BlockSpec((CONV_WIDTH, tf), lambda i, j: (0, j + nf)),
            pl.BlockSpec((1, tf), lambda i, j: (0, j)),
            pl.BlockSpec((1, tf), lambda i, j: (0, j + nf)),
            pl.BlockSpec((tf, d), lambda i, j: (j, 0)),
        ],
        out_specs=pl.BlockSpec((tm, d), lambda i, j: (i, 0)),
        out_shape=jax.ShapeDtypeStruct((s, d), F32),
        scratch_shapes=[pltpu.VMEM((tm + 2 * HALO, d), BF16)],
        compiler_params=_params("parallel", "arbitrary"),
        name="conv_ffn",
    )(x, x, x, g.reshape(1, d), w_up, w_up, conv_w, conv_w, conv_b.reshape(1, -1), conv_b.reshape(1, -1), w_down)


def _rope_tables(positions, dim):
    inv_freq = 1.0 / (ROPE_THETA ** (jnp.arange(0, dim, 2, dtype=F32) / dim))
    ang = positions.astype(F32)[:, None] * inv_freq
    cos = jnp.cos(ang)
    sin = jnp.sin(ang)
    return jnp.concatenate([cos, cos], axis=1), jnp.concatenate([-sin, sin], axis=1)


def kernel(x, positions, norm_mix, w_in, b_gate, norm_q, w_uq, norm_kv, w_ukv, w_oa, w_ob, w_out,
           norm_ffn, w_up, conv_w, conv_b, w_down, norm_final):
    batch, s, d = x.shape
    assert batch == 1
    depth = w_in.shape[0]
    xs = x.reshape(s, d)
    pos = positions.reshape(s)

    cos_a, sin_a = _rope_tables(pos, HEAD_DIM_A)
    scale_a = HEAD_DIM_A ** -0.5 * math.log2(math.e)
    cos_qk = jnp.stack([cos_a * scale_a, cos_a])
    sin_qk = jnp.stack([sin_a * scale_a, sin_a])
    cos_b, sin_b = _rope_tables(pos, QK_ROPE)
    pad = ((0, 0), (0, LANES - QK_ROPE))
    cos_bp, sin_bp = jnp.pad(cos_b, pad), jnp.pad(sin_b, pad)
    cos_bt = cos_b[:, QK_ROPE // 2:].T
    sin_bt = sin_b[:, QK_ROPE // 2:].T
    scale_b = QK_DIM_B ** -0.5 * math.log2(math.e)

    o_ql = 3 * WIDTH_A
    o_kvl = o_ql + Q_LORA
    o_kr = o_kvl + KV_LORA
    o_g = o_kr + QK_ROPE
    tn = 512

    for l in range(depth):
        w_in_l = w_in[l].astype(BF16)
        w_kr = jnp.pad(w_in_l[:, o_kr:o_g], ((0, 0), (0, LANES - QK_ROPE)))
        w_g = w_in_l[:, o_g:]

        tm = min(1024, s)
        w_a = w_in_l[:, :o_ql].reshape(d, N_SEC, N_GROUPS_A, GROUP_WIDTH_A).transpose(0, 2, 1, 3)
        w_cat = jnp.concatenate([w_a.reshape(d, o_ql), w_in_l[:, o_ql:o_kr]], axis=1)
        *a_groups, lat, kr, h = in_proj(xs, norm_mix[l], w_cat, w_kr, cos_qk, sin_qk, cos_bp, sin_bp, tm=tm)
        oa = dilated_attention(a_groups)

        cq_t, ckv, ckv_t = latent_norm(lat, norm_q[l], norm_kv[l])
        w_uq_t = w_uq[l].T.astype(BF16)
        w_ukv_l = w_ukv[l].reshape(KV_LORA, N_HEADS_B, QK_NOPE + V_DIM)
        w_uk = w_ukv_l[:, :, :QK_NOPE].reshape(KV_LORA, N_HEADS_B * QK_NOPE).astype(BF16)
        w_uv_t = w_ukv_l[:, :, QK_NOPE:].reshape(KV_LORA, N_HEADS_B * V_DIM).T.astype(BF16)
        heads_per_tile = 4
        tq_rows = heads_per_tile * QK_DIM_B
        qt_b = matmul(
            w_uq_t, cq_t, functools.partial(_rope_rows_epilogue, scale=scale_b), (cos_bt, sin_bt),
            [pl.BlockSpec((QK_ROPE // 2, tm), lambda i, j: (0, j))] * 2,
            jax.ShapeDtypeStruct((N_HEADS_B * QK_DIM_B, s), BF16), pl.BlockSpec((tq_rows, tm), lambda i, j: (i, j)),
            tm=tq_rows, tn=tm, name="mla_q_proj").reshape(N_HEADS_B, QK_DIM_B, s)
        vt_b = matmul(
            w_uv_t, ckv_t, _store_epilogue, (), [],
            jax.ShapeDtypeStruct((N_HEADS_B * V_DIM, s), BF16), pl.BlockSpec((tm, tm), lambda i, j: (i, j)),
            tm=tm, tn=tm, name="mla_v_proj").reshape(N_HEADS_B, V_DIM, s)
        k_nope = matmul(
            ckv, w_uk, _store_epilogue, (), [],
            jax.ShapeDtypeStruct((s, N_HEADS_B * QK_NOPE), BF16), pl.BlockSpec((tm, tn), lambda i, j: (i, j)),
            tm=tm, tn=tn, name="mla_k_proj")
        ob = dense_attention(qt_b, k_nope, kr, vt_b)

        merged = gated_merge(oa, ob, h, w_oa[l].astype(BF16), w_ob[l].astype(BF16), w_g,
                             b_gate[l].reshape(1, -1))
        xs = matmul(
            merged, w_out[l].astype(BF16), _residual_epilogue, (xs,),
            [pl.BlockSpec((tm, tn), lambda i, j: (i, j))],
            jax.ShapeDtypeStruct((s, d), F32), pl.BlockSpec((tm, tn), lambda i, j: (i, j)),
            tm=tm, tn=tn, name="proj_out")

        xs = conv_ffn(xs, norm_ffn[l], w_up[l].astype(BF16), conv_w[l], conv_b[l], w_down[l].astype(BF16))

    out = rmsnorm(xs, norm_final, F32)
    return out.reshape(batch, s, d)
```

```python
import functools
import math

import jax
import jax.numpy as jnp
from jax import lax
from jax.experimental import pallas as pl
from jax.experimental.pallas import tpu as pltpu

F32 = jnp.float32
BF16 = jnp.bfloat16

HEAD_DIM_A = 128
HEADS_PER_GROUP_A = 4
DILATED_GROUPS = ((128, 1), (512, 4), (2048, 16))
GROUP_WIDTH_A = HEADS_PER_GROUP_A * HEAD_DIM_A
WIDTH_A = len(DILATED_GROUPS) * GROUP_WIDTH_A
N_HEADS_B = 16
QK_NOPE = 128
QK_ROPE = 64
V_DIM = 128
Q_LORA = 512
KV_LORA = 512
QK_DIM_B = QK_NOPE + QK_ROPE
ROPE_THETA = 10000.0
EPS = 1e-6
CONV_WIDTH = 3

LANES = 128
MASK_VALUE = -1e30
VMEM_LIMIT = 56 * 1024 * 1024


def _params(*sem):
    return pltpu.CompilerParams(dimension_semantics=sem, vmem_limit_bytes=VMEM_LIMIT)


def _rmsnorm_body(x_ref, g_ref, o_ref):
    x = x_ref[...].astype(F32)
    ms = jnp.mean(x * x, axis=-1, keepdims=True)
    o_ref[...] = (x * lax.rsqrt(ms + EPS) * g_ref[...]).astype(o_ref.dtype)


def rmsnorm(x, g, out_dtype, *, tm=512):
    s, width = x.shape
    tm = min(tm, s)
    return pl.pallas_call(
        _rmsnorm_body,
        grid=(s // tm,),
        in_specs=[
            pl.BlockSpec((tm, width), lambda i: (i, 0)),
            pl.BlockSpec((1, width), lambda i: (0, 0)),
        ],
        out_specs=pl.BlockSpec((tm, width), lambda i: (i, 0)),
        out_shape=jax.ShapeDtypeStruct((s, width), out_dtype),
        compiler_params=_params("parallel"),
        name="rmsnorm",
    )(x, g.reshape(1, width).astype(F32))


def _mm_body(a_ref, b_ref, *rest, epilogue, n_extra):
    acc = jnp.dot(a_ref[...], b_ref[...], preferred_element_type=F32)
    epilogue(acc, rest[:n_extra], rest[n_extra:])


def matmul(a, b, epilogue, extra, extra_specs, out_shapes, out_specs, *, tm, tn, name):
    m, k = a.shape
    n = b.shape[1]
    tm = min(tm, m)
    grid = (m // tm, n // tn)
    return pl.pallas_call(
        functools.partial(_mm_body, epilogue=epilogue, n_extra=len(extra)),
        grid=grid,
        in_specs=[
            pl.BlockSpec((tm, k), lambda i, j: (i, 0)),
            pl.BlockSpec((k, tn), lambda i, j: (0, j)),
            *extra_specs,
        ],
        out_specs=out_specs,
        out_shape=out_shapes,
        compiler_params=_params("parallel", "arbitrary"),
        name=name,
    )(a, b, *extra)


def _rope128_epilogue(acc, extra, outs):
    cos = extra[0][0]
    sin = extra[1][0]
    (o_ref,) = outs
    for c in range(acc.shape[1] // LANES):
        x = acc[:, c * LANES:(c + 1) * LANES]
        y = x * cos + pltpu.roll(x, LANES // 2, 1) * sin
        o_ref[:, c * LANES:(c + 1) * LANES] = y.astype(o_ref.dtype)


def _store_epilogue(acc, extra, outs):
    outs[0][...] = acc.astype(outs[0].dtype)


def _rope64_epilogue(acc, extra, outs):
    cos = extra[0][...]
    sin = extra[1][...]
    lane = lax.broadcasted_iota(jnp.int32, acc.shape, 1)
    half = QK_ROPE // 2
    rot = jnp.where(lane < half, pltpu.roll(acc, LANES - half, 1), pltpu.roll(acc, half, 1))
    outs[0][...] = (acc * cos + rot * sin).astype(outs[0].dtype)


def _residual_epilogue(acc, extra, outs):
    outs[0][...] = extra[0][...] + acc


IN_TN = GROUP_WIDTH_A
N_SEC = 3
N_GROUPS_A = len(DILATED_GROUPS)
N_A_TILES = N_GROUPS_A * N_SEC
N_LAT_TILES = (Q_LORA + KV_LORA) // IN_TN
CHUNKS = IN_TN // LANES


def _in_proj_body(x_ref, g_ref, w_ref, wkr_ref, cosa_ref, sina_ref, cosb_ref, sinb_ref, *rest):
    a_refs = rest[:N_GROUPS_A]
    lat_ref, kr_ref, h_ref, y_sc = rest[N_GROUPS_A:]
    j = pl.program_id(1)
    tm = x_ref.shape[0]

    @pl.when(j == 0)
    def _():
        x = x_ref[...]
        ms = jnp.mean(x * x, axis=-1, keepdims=True)
        h = (x * lax.rsqrt(ms + EPS) * g_ref[...]).astype(h_ref.dtype)
        h_ref[...] = h
        _rope64_epilogue(jnp.dot(h, wkr_ref[...], preferred_element_type=F32), (cosb_ref, sinb_ref), (kr_ref,))

    acc = jnp.dot(h_ref[...], w_ref[...], preferred_element_type=F32)
    sec = j % N_SEC
    is_a = j < N_A_TILES

    @pl.when(jnp.logical_and(is_a, sec < 2))
    def _():
        cos = cosa_ref[0]
        sin = sina_ref[0]
        for c in range(CHUNKS):
            x = acc[:, c * LANES:(c + 1) * LANES]
            y_sc[c] = x * cos + pltpu.roll(x, LANES // 2, 1) * sin

    @pl.when(jnp.logical_and(is_a, sec == 2))
    def _():
        for c in range(CHUNKS):
            y_sc[c] = acc[:, c * LANES:(c + 1) * LANES]

    for gi, (_, dilation) in enumerate(DILATED_GROUPS):
        @pl.when(jnp.logical_and(j >= gi * N_SEC, j < (gi + 1) * N_SEC))
        def _(a_ref=a_refs[gi], dilation=dilation):
            n = tm // dilation
            for r in range(dilation):
                for c in range(CHUNKS):
                    rows = y_sc[c, pl.ds(r, n, stride=dilation), :] if dilation > 1 else y_sc[c]
                    a_ref[r, :, c * LANES:(c + 1) * LANES] = rows.astype(a_ref.dtype)

    @pl.when(j >= N_A_TILES)
    def _():
        lat_ref[...] = acc


def in_proj(x, g, w_cat, w_kr, cos_qk, sin_qk, cos_bp, sin_bp, *, tm=1024):
    s, d = x.shape
    tm = min(tm, s)
    tn = IN_TN
    table = pl.BlockSpec((1, tm, LANES), lambda i, j: (jnp.where(j % N_SEC == 0, 0, 1), i, 0))

    def a_spec(gi, dilation):
        return pl.BlockSpec((dilation, tm // dilation, tn),
                            lambda i, j: (0, i, jnp.clip(j - gi * N_SEC, 0, N_SEC - 1)))

    return pl.pallas_call(
        _in_proj_body,
        grid=(s // tm, N_A_TILES + N_LAT_TILES),
        in_specs=[
            pl.BlockSpec((tm, d), lambda i, j: (i, 0)),
            pl.BlockSpec((1, d), lambda i, j: (0, 0)),
            pl.BlockSpec((d, tn), lambda i, j: (0, j)),
            pl.BlockSpec((d, LANES), lambda i, j: (0, 0)),
            table,
            table,
            pl.BlockSpec((tm, LANES), lambda i, j: (i, 0)),
            pl.BlockSpec((tm, LANES), lambda i, j: (i, 0)),
        ],
        out_specs=[
            *[a_spec(gi, dilation) for gi, (_, dilation) in enumerate(DILATED_GROUPS)],
            pl.BlockSpec((tm, tn), lambda i, j: (i, jnp.clip(j - N_A_TILES, 0, N_LAT_TILES - 1))),
            pl.BlockSpec((tm, LANES), lambda i, j: (i, 0)),
            pl.BlockSpec((tm, d), lambda i, j: (i, 0)),
        ],
        out_shape=[
            *[jax.ShapeDtypeStruct((dilation, s // dilation, N_SEC * tn), BF16) for _, dilation in DILATED_GROUPS],
            jax.ShapeDtypeStruct((s, Q_LORA + KV_LORA), F32),
            jax.ShapeDtypeStruct((s, LANES), BF16),
            jax.ShapeDtypeStruct((s, d), BF16),
        ],
        scratch_shapes=[pltpu.VMEM((CHUNKS, tm, LANES), F32)],
        compiler_params=_params("arbitrary", "arbitrary"),
        name="in_proj",
    )(x, g.reshape(1, d), w_cat, w_kr, cos_qk, sin_qk, cos_bp, sin_bp)


def _latent_norm_body(lat_ref, gq_ref, gkv_ref, cqt_ref, ckv_ref, ckvt_ref):
    def norm(x, g):
        ms = jnp.mean(x * x, axis=-1, keepdims=True)
        return x * lax.rsqrt(ms + EPS) * g

    cq = norm(lat_ref[:, :Q_LORA], gq_ref[...])
    ckv = norm(lat_ref[:, Q_LORA:], gkv_ref[...])
    cqt_ref[...] = cq.T.astype(cqt_ref.dtype)
    ckv_ref[...] = ckv.astype(ckv_ref.dtype)
    ckvt_ref[...] = ckv.T.astype(ckvt_ref.dtype)


def latent_norm(lat, g_q, g_kv, *, tm=512):
    s = lat.shape[0]
    tm = min(tm, s)
    return pl.pallas_call(
        _latent_norm_body,
        grid=(s // tm,),
        in_specs=[
            pl.BlockSpec((tm, Q_LORA + KV_LORA), lambda i: (i, 0)),
            pl.BlockSpec((1, Q_LORA), lambda i: (0, 0)),
            pl.BlockSpec((1, KV_LORA), lambda i: (0, 0)),
        ],
        out_specs=[
            pl.BlockSpec((Q_LORA, tm), lambda i: (0, i)),
            pl.BlockSpec((tm, KV_LORA), lambda i: (i, 0)),
            pl.BlockSpec((KV_LORA, tm), lambda i: (0, i)),
        ],
        out_shape=[
            jax.ShapeDtypeStruct((Q_LORA, s), BF16),
            jax.ShapeDtypeStruct((s, KV_LORA), BF16),
            jax.ShapeDtypeStruct((KV_LORA, s), BF16),
        ],
        compiler_params=_params("parallel"),
        name="latent_norm",
    )(lat, g_q.reshape(1, -1), g_kv.reshape(1, -1))


def _rope_rows_epilogue(acc, extra, outs, *, scale):
    cos = extra[0][...] * scale
    sin = extra[1][...] * scale
    (o_ref,) = outs
    half = QK_ROPE // 2
    for hh in range(acc.shape[0] // QK_DIM_B):
        r0 = hh * QK_DIM_B
        r1 = r0 + QK_NOPE
        x1 = acc[r1:r1 + half]
        x2 = acc[r1 + half:r0 + QK_DIM_B]
        o_ref[r0:r1, :] = (acc[r0:r1] * scale).astype(o_ref.dtype)
        o_ref[r1:r1 + half, :] = (x1 * cos - x2 * sin).astype(o_ref.dtype)
        o_ref[r1 + half:r0 + QK_DIM_B, :] = (x2 * cos + x1 * sin).astype(o_ref.dtype)


def _flash_body(q_ref, kn_ref, kr_ref, vt_ref, o_ref, k_ref, s0_ref, s1_ref, acc_ref, *, tk):
    @pl.when(pl.program_id(1) == 0)
    def _():
        k_ref[:, :QK_NOPE] = kn_ref[...]
        k_ref[:, QK_NOPE:] = kr_ref[:, :QK_ROPE]

    qt = q_ref[...]
    tq = qt.shape[1]
    n_kv = k_ref.shape[0] // tk

    def scores(j, s_ref):
        off = pl.multiple_of(j * tk, tk)
        k = k_ref[pl.ds(off, tk), :]
        st = jnp.dot(k, qt, preferred_element_type=F32)
        s_ref[...] = st
        return jnp.max(st, axis=0, keepdims=True)

    acc_ref[...] = jnp.zeros(acc_ref.shape, F32)

    def half(j, m, l, mx, cur_ref, nxt_ref):
        mx_next = scores(jnp.minimum(j + 1, n_kv - 1), nxt_ref)
        off = pl.multiple_of(j * tk, tk)
        vt = vt_ref[:, pl.ds(off, tk)]
        m_new = jnp.maximum(m, mx)
        alpha = jnp.exp2(m - m_new)
        p = jnp.exp2(cur_ref[...] - m_new)
        l = alpha * l + jnp.sum(p, axis=0, keepdims=True)
        acc_ref[...] = alpha * acc_ref[...] + jnp.dot(vt, p.astype(vt.dtype), preferred_element_type=F32)
        return m_new, l, mx_next

    def step(jj, carry):
        m, l, mx = carry
        m, l, mx = half(2 * jj, m, l, mx, s0_ref, s1_ref)
        m, l, mx = half(2 * jj + 1, m, l, mx, s1_ref, s0_ref)
        return m, l, mx

    m0 = jnp.full((1, tq), MASK_VALUE, F32)
    l0 = jnp.zeros((1, tq), F32)
    _, l, _ = lax.fori_loop(0, n_kv // 2, step, (m0, l0, scores(0, s0_ref)))
    o_ref[...] = (acc_ref[...] / l).T.astype(o_ref.dtype)


def dense_attention(qt, k_nope, kr, vt, *, tq=1024, tk=1024):
    h, dq, s = qt.shape
    dv = vt.shape[1]
    tq = min(tq, s)
    tk = min(tk, s // 2)
    assert (s // tk) % 2 == 0
    return pl.pallas_call(
        functools.partial(_flash_body, tk=tk),
        grid=(h, s // tq),
        in_specs=[
            pl.BlockSpec((None, dq, tq), lambda hh, i: (hh, 0, i)),
            pl.BlockSpec((s, QK_NOPE), lambda hh, i: (0, hh)),
            pl.BlockSpec((s, LANES), lambda hh, i: (0, 0)),
            pl.BlockSpec((None, dv, s), lambda hh, i: (hh, 0, 0)),
        ],
        out_specs=pl.BlockSpec((tq, dv), lambda hh, i: (i, hh)),
        out_shape=jax.ShapeDtypeStruct((s, h * dv), BF16),
        scratch_shapes=[
            pltpu.VMEM((s, dq), BF16),
            pltpu.VMEM((tk, tq), F32),
            pltpu.VMEM((tk, tq), F32),
            pltpu.VMEM((dv, tq), F32),
        ],
        compiler_params=_params("arbitrary", "arbitrary"),
        name="mla_flash",
    )(qt, k_nope, kr, vt)


HALO_A = 64
assert all(window // (2 * dilation) == HALO_A for window, dilation in DILATED_GROUPS)


def _dilated_group_body(q_ref, kp_ref, km_ref, kn_ref, vp_ref, vm_ref, vn_ref, o_ref, lse_ref, k_sc, v_sc):
    b = pl.program_id(1)
    nb = pl.num_programs(1)
    t = q_ref.shape[0]
    w = t + 2 * HALO_A
    k_sc[:HALO_A] = kp_ref[...]
    k_sc[HALO_A:HALO_A + t] = km_ref[...]
    k_sc[HALO_A + t:] = kn_ref[...]
    v_sc[:HALO_A] = vp_ref[...]
    v_sc[HALO_A:HALO_A + t] = vm_ref[...]
    v_sc[HALO_A + t:] = vn_ref[...]
    qry = lax.broadcasted_iota(jnp.int32, (t, w), 0)
    key = lax.broadcasted_iota(jnp.int32, (t, w), 1)
    dist = key - HALO_A - qry
    lo = jnp.where(b > 0, 0, HALO_A)
    hi = jnp.where(b < nb - 1, w, HALO_A + t)
    valid = (jnp.abs(dist) <= HALO_A) & (key >= lo) & (key < hi)
    lane = lax.broadcasted_iota(jnp.int32, (t, LANES), 1)
    lse = jnp.zeros((t, LANES), F32)
    for hh in range(HEADS_PER_GROUP_A):
        sl = slice(hh * HEAD_DIM_A, (hh + 1) * HEAD_DIM_A)
        s = lax.dot_general(q_ref[:, sl], k_sc[:, sl], (((1,), (1,)), ((), ())), preferred_element_type=F32)
        s = jnp.where(valid, s, MASK_VALUE)
        m = jnp.max(s, axis=1, keepdims=True)
        p = jnp.exp2(s - m)
        l = jnp.sum(p, axis=1, keepdims=True)
        o = jnp.dot(p.astype(v_sc.dtype), v_sc[:, sl], preferred_element_type=F32)
        o_ref[:, sl] = o / l
        lse = jnp.where(lane == hh, m + jnp.log2(l), lse)
    lse_ref[...] = lse


def dilated_group(a, *, t=256):
    dilation, length, _ = a.shape
    t = min(t, length)
    hb = t // HALO_A
    n_halo = length // HALO_A

    def main(sec):
        return pl.BlockSpec((None, t, GROUP_WIDTH_A), lambda r, b: (r, b, sec))

    def prev(sec):
        return pl.BlockSpec((None, HALO_A, GROUP_WIDTH_A), lambda r, b: (r, jnp.maximum(b * hb - 1, 0), sec))

    def nxt(sec):
        return pl.BlockSpec((None, HALO_A, GROUP_WIDTH_A),
                            lambda r, b: (r, jnp.minimum((b + 1) * hb, n_halo - 1), sec))

    return pl.pallas_call(
        _dilated_group_body,
        grid=(dilation, length // t),
        in_specs=[main(0), prev(1), main(1), nxt(1), prev(2), main(2), nxt(2)],
        out_specs=[
            pl.BlockSpec((None, t, GROUP_WIDTH_A), lambda r, b: (r, b, 0)),
            pl.BlockSpec((None, t, LANES), lambda r, b: (r, b, 0)),
        ],
        out_shape=[
            jax.ShapeDtypeStruct((dilation, length, GROUP_WIDTH_A), F32),
            jax.ShapeDtypeStruct((dilation, length, LANES), F32),
        ],
        scratch_shapes=[
            pltpu.VMEM((t + 2 * HALO_A, GROUP_WIDTH_A), BF16),
            pltpu.VMEM((t + 2 * HALO_A, GROUP_WIDTH_A), BF16),
        ],
        compiler_params=_params("parallel", "parallel"),
        name="dilated_group",
    )(a, a, a, a, a, a, a)


def _dilated_combine_body(*refs):
    n = N_GROUPS_A
    o_refs, l_refs = refs[:n], refs[n:2 * n]
    out_ref, o_sc, l_sc = refs[2 * n:]
    tm = out_ref.shape[0]
    for gi, (_, dilation) in enumerate(DILATED_GROUPS):
        rows = tm // dilation
        for r in range(dilation):
            dst = pl.ds(r, rows, stride=dilation) if dilation > 1 else slice(None)
            l_sc[gi, dst, :] = l_refs[gi][r]
            for hh in range(HEADS_PER_GROUP_A):
                o_sc[gi, hh, dst, :] = o_refs[gi][r, :, hh * HEAD_DIM_A:(hh + 1) * HEAD_DIM_A]
    for hh in range(HEADS_PER_GROUP_A):
        ls = [l_sc[gi, :, hh:hh + 1] for gi in range(n)]
        m = functools.reduce(jnp.maximum, ls)
        ws = [jnp.exp2(x - m) for x in ls]
        den = functools.reduce(jnp.add, ws)
        acc = functools.reduce(jnp.add, [ws[gi] * o_sc[gi, hh] for gi in range(n)])
        out_ref[:, hh * HEAD_DIM_A:(hh + 1) * HEAD_DIM_A] = (acc / den).astype(out_ref.dtype)


def dilated_attention(a_groups, *, tm=1024):
    s = a_groups[0].shape[0] * a_groups[0].shape[1]
    tm = min(tm, s)
    outs, lses = zip(*[dilated_group(a) for a in a_groups])

    def spec(dilation, width):
        return pl.BlockSpec((dilation, tm // dilation, width), lambda i: (0, i, 0))

    return pl.pallas_call(
        _dilated_combine_body,
        grid=(s // tm,),
        in_specs=[spec(d, GROUP_WIDTH_A) for _, d in DILATED_GROUPS] + [spec(d, LANES) for _, d in DILATED_GROUPS],
        out_specs=pl.BlockSpec((tm, GROUP_WIDTH_A), lambda i: (i, 0)),
        out_shape=jax.ShapeDtypeStruct((s, GROUP_WIDTH_A), BF16),
        scratch_shapes=[
            pltpu.VMEM((N_GROUPS_A, HEADS_PER_GROUP_A, tm, HEAD_DIM_A), F32),
            pltpu.VMEM((N_GROUPS_A, tm, LANES), F32),
        ],
        compiler_params=_params("parallel"),
        name="dilated_combine",
    )(*outs, *lses)


def _merge_body(oa_ref, ob_ref, h_ref, woa_ref, wob_ref, wga_ref, wgb_ref, ba_ref, bb_ref, o_ref):
    h = h_ref[...]

    def gate(wg_ref, b_ref):
        z = jnp.dot(h, wg_ref[...], preferred_element_type=F32) + b_ref[...]
        return 1.0 / (1.0 + jnp.exp(-z))

    ya = jnp.dot(oa_ref[...], woa_ref[...], preferred_element_type=F32)
    yb = jnp.dot(ob_ref[...], wob_ref[...], preferred_element_type=F32)
    o_ref[...] = (gate(wga_ref, ba_ref) * ya + gate(wgb_ref, bb_ref) * yb).astype(o_ref.dtype)


def gated_merge(oa, ob, h, w_oa, w_ob, w_g, b_g, *, tm=1024, tn=512):
    s = oa.shape[0]
    d = w_oa.shape[1]
    tm = min(tm, s)
    nj = d // tn
    return pl.pallas_call(
        _merge_body,
        grid=(s // tm, nj),
        in_specs=[
            pl.BlockSpec((tm, oa.shape[1]), lambda i, j: (i, 0)),
            pl.BlockSpec((tm, ob.shape[1]), lambda i, j: (i, 0)),
            pl.BlockSpec((tm, h.shape[1]), lambda i, j: (i, 0)),
            pl.BlockSpec((w_oa.shape[0], tn), lambda i, j: (0, j)),
            pl.BlockSpec((w_ob.shape[0], tn), lambda i, j: (0, j)),
            pl.BlockSpec((w_g.shape[0], tn), lambda i, j: (0, j)),
            pl.BlockSpec((w_g.shape[0], tn), lambda i, j: (0, j + nj)),
            pl.BlockSpec((1, tn), lambda i, j: (0, j)),
            pl.BlockSpec((1, tn), lambda i, j: (0, j + nj)),
        ],
        out_specs=pl.BlockSpec((tm, tn), lambda i, j: (i, j)),
        out_shape=jax.ShapeDtypeStruct((s, d), BF16),
        compiler_params=_params("parallel", "arbitrary"),
        name="gated_merge",
    )(oa, ob, h, w_oa, w_ob, w_g, w_g, b_g, b_g)


HALO = 16


def _ffn_body(x_ref, xp_ref, xn_ref, g_ref, wa_ref, wb_ref, cwa_ref, cwb_ref, cba_ref, cbb_ref, wd_ref,
              o_ref, h_sc):
    i = pl.program_id(0)
    j = pl.program_id(1)
    tm = x_ref.shape[0]

    def norm(x):
        ms = jnp.mean(x * x, axis=-1, keepdims=True)
        return x * lax.rsqrt(ms + EPS) * g_ref[...]

    @pl.when(j == 0)
    def _():
        x = x_ref[...]
        o_ref[...] = x
        h_sc[HALO:HALO + tm, :] = norm(x).astype(h_sc.dtype)
        hp = jnp.where(i > 0, norm(xp_ref[...]), 0.0)
        hn = jnp.where(i < pl.num_programs(0) - 1, norm(xn_ref[...]), 0.0)
        h_sc[:HALO, :] = hp.astype(h_sc.dtype)
        h_sc[HALO + tm:, :] = hn.astype(h_sc.dtype)

    h = h_sc[...]
    rows = tm + 2 * HALO

    def conv(w_ref, cw_ref, cb_ref):
        up = jnp.dot(h, w_ref[...], preferred_element_type=F32)
        cw = cw_ref[...]
        prev = pltpu.roll(up, 1, 0)
        nxt = pltpu.roll(up, rows - 1, 0)
        u = cb_ref[...] + prev * cw[0:1] + up * cw[1:2] + nxt * cw[2:3]
        return u[HALO:HALO + tm]

    ua = conv(wa_ref, cwa_ref, cba_ref)
    ub = conv(wb_ref, cwb_ref, cbb_ref)
    act = (ua / (1.0 + jnp.exp(-ua))) * ub
    o_ref[...] += jnp.dot(act.astype(wd_ref.dtype), wd_ref[...], preferred_element_type=F32)


def conv_ffn(x, g, w_up, conv_w, conv_b, w_down, *, tm=1024, tf=512):
    s, d = x.shape
    f = w_down.shape[0]
    tm = min(tm, s)
    nf = f // tf
    hb = tm // HALO
    n_halo_blocks = s // HALO
    return pl.pallas_call(
        _ffn_body,
        grid=(s // tm, nf),
        in_specs=[
            pl.BlockSpec((tm, d), lambda i, j: (i, 0)),
            pl.BlockSpec((HALO, d), lambda i, j: (jnp.maximum(i * hb - 1, 0), 0)),
            pl.BlockSpec((HALO, d), lambda i, j: (jnp.minimum((i + 1) * hb, n_halo_blocks - 1), 0)),
            pl.BlockSpec((1, d), lambda i, j: (0, 0)),
            pl.BlockSpec((d, tf), lambda i, j: (0, j)),
            pl.BlockSpec((d, tf), lambda i, j: (0, j + nf)),
            pl.BlockSpec((CONV_WIDTH, tf), lambda i, j: (0, j)),
            pl.BlockSpec((CONV_WIDTH, tf), lambda i, j: (0, j + nf)),
            pl.BlockSpec((1, tf), lambda i, j: (0, j)),
            pl.BlockSpec((1, tf), lambda i, j: (0, j + nf)),
            pl.BlockSpec((tf, d), lambda i, j: (j, 0)),
        ],
        out_specs=pl.BlockSpec((tm, d), lambda i, j: (i, 0)),
        out_shape=jax.ShapeDtypeStruct((s, d), F32),
        scratch_shapes=[pltpu.VMEM((tm + 2 * HALO, d), BF16)],
        compiler_params=_params("parallel", "arbitrary"),
        name="conv_ffn",
    )(x, x, x, g.reshape(1, d), w_up, w_up, conv_w, conv_w, conv_b.reshape(1, -1), conv_b.reshape(1, -1), w_down)


def _rope_tables(positions, dim):
    inv_freq = 1.0 / (ROPE_THETA ** (jnp.arange(0, dim, 2, dtype=F32) / dim))
    ang = positions.astype(F32)[:, None] * inv_freq
    cos = jnp.cos(ang)
    sin = jnp.sin(ang)
    return jnp.concatenate([cos, cos], axis=1), jnp.concatenate([-sin, sin], axis=1)


def kernel(x, positions, norm_mix, w_in, b_gate, norm_q, w_uq, norm_kv, w_ukv, w_oa, w_ob, w_out,
           norm_ffn, w_up, conv_w, conv_b, w_down, norm_final):
    batch, s, d = x.shape
    assert batch == 1
    depth = w_in.shape[0]
    xs = x.reshape(s, d)
    pos = positions.reshape(s)

    cos_a, sin_a = _rope_tables(pos, HEAD_DIM_A)
    scale_a = HEAD_DIM_A ** -0.5 * math.log2(math.e)
    cos_qk = jnp.stack([cos_a * scale_a, cos_a])
    sin_qk = jnp.stack([sin_a * scale_a, sin_a])
    cos_b, sin_b = _rope_tables(pos, QK_ROPE)
    pad = ((0, 0), (0, LANES - QK_ROPE))
    cos_bp, sin_bp = jnp.pad(cos_b, pad), jnp.pad(sin_b, pad)
    cos_bt = cos_b[:, QK_ROPE // 2:].T
    sin_bt = sin_b[:, QK_ROPE // 2:].T
    scale_b = QK_DIM_B ** -0.5 * math.log2(math.e)

    o_ql = 3 * WIDTH_A
    o_kvl = o_ql + Q_LORA
    o_kr = o_kvl + KV_LORA
    o_g = o_kr + QK_ROPE
    tn = 512

    for l in range(depth):
        w_in_l = w_in[l].astype(BF16)
        w_kr = jnp.pad(w_in_l[:, o_kr:o_g], ((0, 0), (0, LANES - QK_ROPE)))
        w_g = w_in_l[:, o_g:]

        tm = min(1024, s)
        w_a = w_in_l[:, :o_ql].reshape(d, N_SEC, N_GROUPS_A, GROUP_WIDTH_A).transpose(0, 2, 1, 3)
        w_cat = jnp.concatenate([w_a.reshape(d, o_ql), w_in_l[:, o_ql:o_kr]], axis=1)
        *a_groups, lat, kr, h = in_proj(xs, norm_mix[l], w_cat, w_kr, cos_qk, sin_qk, cos_bp, sin_bp, tm=tm)
        oa = dilated_attention(a_groups)

        cq_t, ckv, ckv_t = latent_norm(lat, norm_q[l], norm_kv[l])
        w_uq_t = w_uq[l].T.astype(BF16)
        w_ukv_l = w_ukv[l].reshape(KV_LORA, N_HEADS_B, QK_NOPE + V_DIM)
        w_uk = w_ukv_l[:, :, :QK_NOPE].reshape(KV_LORA, N_HEADS_B * QK_NOPE).astype(BF16)
        w_uv_t = w_ukv_l[:, :, QK_NOPE:].reshape(KV_LORA, N_HEADS_B * V_DIM).T.astype(BF16)
        heads_per_tile = 4
        tq_rows = heads_per_tile * QK_DIM_B
        qt_b = matmul(
            w_uq_t, cq_t, functools.partial(_rope_rows_epilogue, scale=scale_b), (cos_bt, sin_bt),
            [pl.BlockSpec((QK_ROPE // 2, tm), lambda i, j: (0, j))] * 2,
            jax.ShapeDtypeStruct((N_HEADS_B * QK_DIM_B, s), BF16), pl.BlockSpec((tq_rows, tm), lambda i, j: (i, j)),
            tm=tq_rows, tn=tm, name="mla_q_proj").reshape(N_HEADS_B, QK_DIM_B, s)
        vt_b = matmul(
            w_uv_t, ckv_t, _store_epilogue, (), [],
            jax.ShapeDtypeStruct((N_HEADS_B * V_DIM, s), BF16), pl.BlockSpec((tm, tm), lambda i, j: (i, j)),
            tm=tm, tn=tm, name="mla_v_proj").reshape(N_HEADS_B, V_DIM, s)
        k_nope = matmul(
            ckv, w_uk, _store_epilogue, (), [],
            jax.ShapeDtypeStruct((s, N_HEADS_B * QK_NOPE), BF16), pl.BlockSpec((tm, tn), lambda i, j: (i, j)),
            tm=tm, tn=tn, name="mla_k_proj")
        ob = dense_attention(qt_b, k_nope, kr, vt_b)

        merged = gated_merge(oa, ob, h, w_oa[l].astype(BF16), w_ob[l].astype(BF16), w_g,
                             b_gate[l].reshape(1, -1))
        xs = matmul(
            merged, w_out[l].astype(BF16), _residual_epilogue, (xs,),
            [pl.BlockSpec((tm, tn), lambda i, j: (i, j))],
            jax.ShapeDtypeStruct((s, d), F32), pl.BlockSpec((tm, tn), lambda i, j: (i, j)),
            tm=tm, tn=tn, name="proj_out")

        xs = conv_ffn(xs, norm_ffn[l], w_up[l].astype(BF16), conv_w[l], conv_b[l], w_down[l].astype(BF16))

    out = rmsnorm(xs, norm_final, F32)
    return out.reshape(batch, s, d)
```

```python
import functools
import math

import jax
import jax.numpy as jnp
from jax import lax
from jax.experimental import pallas as pl
from jax.experimental.pallas import tpu as pltpu

F32 = jnp.float32
BF16 = jnp.bfloat16

HEAD_DIM_A = 128
HEADS_PER_GROUP_A = 4
DILATED_GROUPS = ((128, 1), (512, 4), (2048, 16))
GROUP_WIDTH_A = HEADS_PER_GROUP_A * HEAD_DIM_A
WIDTH_A = len(DILATED_GROUPS) * GROUP_WIDTH_A
N_HEADS_B = 16
QK_NOPE = 128
QK_ROPE = 64
V_DIM = 128
Q_LORA = 512
KV_LORA = 512
QK_DIM_B = QK_NOPE + QK_ROPE
ROPE_THETA = 10000.0
EPS = 1e-6
CONV_WIDTH = 3

LANES = 128
MASK_VALUE = -1e30
VMEM_LIMIT = 56 * 1024 * 1024


def _params(*sem):
    return pltpu.CompilerParams(dimension_semantics=sem, vmem_limit_bytes=VMEM_LIMIT)


def _rmsnorm_body(x_ref, g_ref, o_ref):
    x = x_ref[...].astype(F32)
    ms = jnp.mean(x * x, axis=-1, keepdims=True)
    o_ref[...] = (x * lax.rsqrt(ms + EPS) * g_ref[...]).astype(o_ref.dtype)


def rmsnorm(x, g, out_dtype, *, tm=512):
    s, width = x.shape
    tm = min(tm, s)
    return pl.pallas_call(
        _rmsnorm_body,
        grid=(s // tm,),
        in_specs=[
            pl.BlockSpec((tm, width), lambda i: (i, 0)),
            pl.BlockSpec((1, width), lambda i: (0, 0)),
        ],
        out_specs=pl.BlockSpec((tm, width), lambda i: (i, 0)),
        out_shape=jax.ShapeDtypeStruct((s, width), out_dtype),
        compiler_params=_params("parallel"),
        name="rmsnorm",
    )(x, g.reshape(1, width).astype(F32))


def _mm_body(a_ref, b_ref, *rest, epilogue, n_extra):
    acc = jnp.dot(a_ref[...], b_ref[...], preferred_element_type=F32)
    epilogue(acc, rest[:n_extra], rest[n_extra:])


def matmul(a, b, epilogue, extra, extra_specs, out_shapes, out_specs, *, tm, tn, name):
    m, k = a.shape
    n = b.shape[1]
    tm = min(tm, m)
    grid = (m // tm, n // tn)
    return pl.pallas_call(
        functools.partial(_mm_body, epilogue=epilogue, n_extra=len(extra)),
        grid=grid,
        in_specs=[
            pl.BlockSpec((tm, k), lambda i, j: (i, 0)),
            pl.BlockSpec((k, tn), lambda i, j: (0, j)),
            *extra_specs,
        ],
        out_specs=out_specs,
        out_shape=out_shapes,
        compiler_params=_params("parallel", "arbitrary"),
        name=name,
    )(a, b, *extra)


def _rope128_epilogue(acc, extra, outs):
    cos = extra[0][0]
    sin = extra[1][0]
    (o_ref,) = outs
    for c in range(acc.shape[1] // LANES):
        x = acc[:, c * LANES:(c + 1) * LANES]
        y = x * cos + pltpu.roll(x, LANES // 2, 1) * sin
        o_ref[:, c * LANES:(c + 1) * LANES] = y.astype(o_ref.dtype)


def _store_epilogue(acc, extra, outs):
    outs[0][...] = acc.astype(outs[0].dtype)


def _rope64_epilogue(acc, extra, outs):
    cos = extra[0][...]
    sin = extra[1][...]
    lane = lax.broadcasted_iota(jnp.int32, acc.shape, 1)
    half = QK_ROPE // 2
    rot = jnp.where(lane < half, pltpu.roll(acc, LANES - half, 1), pltpu.roll(acc, half, 1))
    outs[0][...] = (acc * cos + rot * sin).astype(outs[0].dtype)


def _residual_epilogue(acc, extra, outs):
    outs[0][...] = extra[0][...] + acc


IN_TN = GROUP_WIDTH_A
N_SEC = 3
N_GROUPS_A = len(DILATED_GROUPS)
N_A_TILES = N_GROUPS_A * N_SEC
N_LAT_TILES = (Q_LORA + KV_LORA) // IN_TN
CHUNKS = IN_TN // LANES


def _in_proj_body(x_ref, g_ref, w_ref, wkr_ref, cosa_ref, sina_ref, cosb_ref, sinb_ref, *rest):
    a_refs = rest[:N_GROUPS_A]
    lat_ref, kr_ref, h_ref, y_sc = rest[N_GROUPS_A:]
    j = pl.program_id(1)
    tm = x_ref.shape[0]

    @pl.when(j == 0)
    def _():
        x = x_ref[...]
        ms = jnp.mean(x * x, axis=-1, keepdims=True)
        h = (x * lax.rsqrt(ms + EPS) * g_ref[...]).astype(h_ref.dtype)
        h_ref[...] = h
        _rope64_epilogue(jnp.dot(h, wkr_ref[...], preferred_element_type=F32), (cosb_ref, sinb_ref), (kr_ref,))

    acc = jnp.dot(h_ref[...], w_ref[...], preferred_element_type=F32)
    sec = j % N_SEC
    is_a = j < N_A_TILES

    @pl.when(jnp.logical_and(is_a, sec < 2))
    def _():
        cos = cosa_ref[0]
        sin = sina_ref[0]
        for c in range(CHUNKS):
            x = acc[:, c * LANES:(c + 1) * LANES]
            y_sc[c] = x * cos + pltpu.roll(x, LANES // 2, 1) * sin

    @pl.when(jnp.logical_and(is_a, sec == 2))
    def _():
        for c in range(CHUNKS):
            y_sc[c] = acc[:, c * LANES:(c + 1) * LANES]

    for gi, (_, dilation) in enumerate(DILATED_GROUPS):
        @pl.when(jnp.logical_and(j >= gi * N_SEC, j < (gi + 1) * N_SEC))
        def _(a_ref=a_refs[gi], dilation=dilation):
            n = tm // dilation
            for r in range(dilation):
                for c in range(CHUNKS):
                    rows = y_sc[c, pl.ds(r, n, stride=dilation), :] if dilation > 1 else y_sc[c]
                    a_ref[r, :, c * LANES:(c + 1) * LANES] = rows.astype(a_ref.dtype)

    @pl.when(j >= N_A_TILES)
    def _():
        lat_ref[...] = acc


def in_proj(x, g, w_cat, w_kr, cos_qk, sin_qk, cos_bp, sin_bp, *, tm=1024):
    s, d = x.shape
    tm = min(tm, s)
    tn = IN_TN
    table = pl.BlockSpec((1, tm, LANES), lambda i, j: (jnp.where(j % N_SEC == 0, 0, 1), i, 0))

    def a_spec(gi, dilation):
        return pl.BlockSpec((dilation, tm // dilation, tn),
                            lambda i, j: (0, i, jnp.clip(j - gi * N_SEC, 0, N_SEC - 1)))

    return pl.pallas_call(
        _in_proj_body,
        grid=(s // tm, N_A_TILES + N_LAT_TILES),
        in_specs=[
            pl.BlockSpec((tm, d), lambda i, j: (i, 0)),
            pl.BlockSpec((1, d), lambda i, j: (0, 0)),
            pl.BlockSpec((d, tn), lambda i, j: (0, j)),
            pl.BlockSpec((d, LANES), lambda i, j: (0, 0)),
            table,
            table,
            pl.BlockSpec((tm, LANES), lambda i, j: (i, 0)),
            pl.BlockSpec((tm, LANES), lambda i, j: (i, 0)),
        ],
        out_specs=[
            *[a_spec(gi, dilation) for gi, (_, dilation) in enumerate(DILATED_GROUPS)],
            pl.BlockSpec((tm, tn), lambda i, j: (i, jnp.clip(j - N_A_TILES, 0, N_LAT_TILES - 1))),
            pl.BlockSpec((tm, LANES), lambda i, j: (i, 0)),
            pl.BlockSpec((tm, d), lambda i, j: (i, 0)),
        ],
        out_shape=[
            *[jax.ShapeDtypeStruct((dilation, s // dilation, N_SEC * tn), BF16) for _, dilation in DILATED_GROUPS],
            jax.ShapeDtypeStruct((s, Q_LORA + KV_LORA), F32),
            jax.ShapeDtypeStruct((s, LANES), BF16),
            jax.ShapeDtypeStruct((s, d), BF16),
        ],
        scratch_shapes=[pltpu.VMEM((CHUNKS, tm, LANES), F32)],
        compiler_params=_params("arbitrary", "arbitrary"),
        name="in_proj",
    )(x, g.reshape(1, d), w_cat, w_kr, cos_qk, sin_qk, cos_bp, sin_bp)


def _latent_norm_body(lat_ref, gq_ref, gkv_ref, cqt_ref, ckv_ref, ckvt_ref):
    def norm(x, g):
        ms = jnp.mean(x * x, axis=-1, keepdims=True)
        return x * lax.rsqrt(ms + EPS) * g

    cq = norm(lat_ref[:, :Q_LORA], gq_ref[...])
    ckv = norm(lat_ref[:, Q_LORA:], gkv_ref[...])
    cqt_ref[...] = cq.T.astype(cqt_ref.dtype)
    ckv_ref[...] = ckv.astype(ckv_ref.dtype)
    ckvt_ref[...] = ckv.T.astype(ckvt_ref.dtype)


def latent_norm(lat, g_q, g_kv, *, tm=512):
    s = lat.shape[0]
    tm = min(tm, s)
    return pl.pallas_call(
        _latent_norm_body,
        grid=(s // tm,),
        in_specs=[
            pl.BlockSpec((tm, Q_LORA + KV_LORA), lambda i: (i, 0)),
            pl.BlockSpec((1, Q_LORA), lambda i: (0, 0)),
            pl.BlockSpec((1, KV_LORA), lambda i: (0, 0)),
        ],
        out_specs=[
            pl.BlockSpec((Q_LORA, tm), lambda i: (0, i)),
            pl.BlockSpec((tm, KV_LORA), lambda i: (i, 0)),
            pl.BlockSpec((KV_LORA, tm), lambda i: (0, i)),
        ],
        out_shape=[
            jax.ShapeDtypeStruct((Q_LORA, s), BF16),
            jax.ShapeDtypeStruct((s, KV_LORA), BF16),
            jax.ShapeDtypeStruct((KV_LORA, s), BF16),
        ],
        compiler_params=_params("parallel"),
        name="latent_norm",
    )(lat, g_q.reshape(1, -1), g_kv.reshape(1, -1))


def _rope_rows_epilogue(acc, extra, outs, *, scale):
    cos = extra[0][...] * scale
    sin = extra[1][...] * scale
    (o_ref,) = outs
    half = QK_ROPE // 2
    for hh in range(acc.shape[0] // QK_DIM_B):
        r0 = hh * QK_DIM_B
        r1 = r0 + QK_NOPE
        x1 = acc[r1:r1 + half]
        x2 = acc[r1 + half:r0 + QK_DIM_B]
        o_ref[r0:r1, :] = (acc[r0:r1] * scale).astype(o_ref.dtype)
        o_ref[r1:r1 + half, :] = (x1 * cos - x2 * sin).astype(o_ref.dtype)
        o_ref[r1 + half:r0 + QK_DIM_B, :] = (x2 * cos + x1 * sin).astype(o_ref.dtype)


def _flash_body(q_ref, kn_ref, kr_ref, vt_ref, o_ref, k_ref, s0_ref, s1_ref, acc_ref, *, tk):
    @pl.when(pl.program_id(1) == 0)
    def _():
        k_ref[:, :QK_NOPE] = kn_ref[...]
        k_ref[:, QK_NOPE:] = kr_ref[:, :QK_ROPE]

    qt = q_ref[...]
    tq = qt.shape[1]
    n_kv = k_ref.shape[0] // tk

    def scores(j, s_ref):
        off = pl.multiple_of(j * tk, tk)
        k = k_ref[pl.ds(off, tk), :]
        st = jnp.dot(k, qt, preferred_element_type=F32)
        s_ref[...] = st
        return jnp.max(st, axis=0, keepdims=True)

    acc_ref[...] = jnp.zeros(acc_ref.shape, F32)

    def half(j, m, l, mx, cur_ref, nxt_ref):
        mx_next = scores(jnp.minimum(j + 1, n_kv - 1), nxt_ref)
        off = pl.multiple_of(j * tk, tk)
        vt = vt_ref[:, pl.ds(off, tk)]
        m_new = jnp.maximum(m, mx)
        alpha = jnp.exp2(m - m_new)
        p = jnp.exp2(cur_ref[...] - m_new)
        l = alpha * l + jnp.sum(p, axis=0, keepdims=True)
        acc_ref[...] = alpha * acc_ref[...] + jnp.dot(vt, p.astype(vt.dtype), preferred_element_type=F32)
        return m_new, l, mx_next

    def step(jj, carry):
        m, l, mx = carry
        m, l, mx = half(2 * jj, m, l, mx, s0_ref, s1_ref)
        m, l, mx = half(2 * jj + 1, m, l, mx, s1_ref, s0_ref)
        return m, l, mx

    m0 = jnp.full((1, tq), MASK_VALUE, F32)
    l0 = jnp.zeros((1, tq), F32)
    _, l, _ = lax.fori_loop(0, n_kv // 2, step, (m0, l0, scores(0, s0_ref)))
    o_ref[...] = (acc_ref[...] / l).T.astype(o_ref.dtype)


def dense_attention(qt, k_nope, kr, vt, *, tq=1024, tk=1024):
    h, dq, s = qt.shape
    dv = vt.shape[1]
    tq = min(tq, s)
    tk = min(tk, s // 2)
    assert (s // tk) % 2 == 0
    return pl.pallas_call(
        functools.partial(_flash_body, tk=tk),
        grid=(h, s // tq),
        in_specs=[
            pl.BlockSpec((None, dq, tq), lambda hh, i: (hh, 0, i)),
            pl.BlockSpec((s, QK_NOPE), lambda hh, i: (0, hh)),
            pl.BlockSpec((s, LANES), lambda hh, i: (0, 0)),
            pl.BlockSpec((None, dv, s), lambda hh, i: (hh, 0, 0)),
        ],
        out_specs=pl.BlockSpec((tq, dv), lambda hh, i: (i, hh)),
        out_shape=jax.ShapeDtypeStruct((s, h * dv), BF16),
        scratch_shapes=[
            pltpu.VMEM((s, dq), BF16),
            pltpu.VMEM((tk, tq), F32),
            pltpu.VMEM((tk, tq), F32),
            pltpu.VMEM((dv, tq), F32),
        ],
        compiler_params=_params("arbitrary", "arbitrary"),
        name="mla_flash",
    )(qt, k_nope, kr, vt)


HALO_A = 64
assert all(window // (2 * dilation) == HALO_A for window, dilation in DILATED_GROUPS)


def _dilated_group_body(q_ref, kp_ref, km_ref, kn_ref, vp_ref, vm_ref, vn_ref, o_ref, lse_ref, k_sc, v_sc):
    b = pl.program_id(1)
    nb = pl.num_programs(1)
    t = q_ref.shape[0]
    w = t + 2 * HALO_A
    k_sc[:HALO_A] = kp_ref[...]
    k_sc[HALO_A:HALO_A + t] = km_ref[...]
    k_sc[HALO_A + t:] = kn_ref[...]
    v_sc[:HALO_A] = vp_ref[...]
    v_sc[HALO_A:HALO_A + t] = vm_ref[...]
    v_sc[HALO_A + t:] = vn_ref[...]
    qry = lax.broadcasted_iota(jnp.int32, (t, w), 0)
    key = lax.broadcasted_iota(jnp.int32, (t, w), 1)
    dist = key - HALO_A - qry
    lo = jnp.where(b > 0, 0, HALO_A)
    hi = jnp.where(b < nb - 1, w, HALO_A + t)
    valid = (jnp.abs(dist) <= HALO_A) & (key >= lo) & (key < hi)
    lane = lax.broadcasted_iota(jnp.int32, (t, LANES), 1)
    lse = jnp.zeros((t, LANES), F32)
    for hh in range(HEADS_PER_GROUP_A):
        sl = slice(hh * HEAD_DIM_A, (hh + 1) * HEAD_DIM_A)
        s = lax.dot_general(q_ref[:, sl], k_sc[:, sl], (((1,), (1,)), ((), ())), preferred_element_type=F32)
        s = jnp.where(valid, s, MASK_VALUE)
        m = jnp.max(s, axis=1, keepdims=True)
        p = jnp.exp2(s - m)
        l = jnp.sum(p, axis=1, keepdims=True)
        o = jnp.dot(p.astype(v_sc.dtype), v_sc[:, sl], preferred_element_type=F32)
        o_ref[:, sl] = o / l
        lse = jnp.where(lane == hh, m + jnp.log2(l), lse)
    lse_ref[...] = lse


def dilated_group(a, *, t=256):
    dilation, length, _ = a.shape
    t = min(t, length)
    hb = t // HALO_A
    n_halo = length // HALO_A

    def main(sec):
        return pl.BlockSpec((None, t, GROUP_WIDTH_A), lambda r, b: (r, b, sec))

    def prev(sec):
        return pl.BlockSpec((None, HALO_A, GROUP_WIDTH_A), lambda r, b: (r, jnp.maximum(b * hb - 1, 0), sec))

    def nxt(sec):
        return pl.BlockSpec((None, HALO_A, GROUP_WIDTH_A),
                            lambda r, b: (r, jnp.minimum((b + 1) * hb, n_halo - 1), sec))

    return pl.pallas_call(
        _dilated_group_body,
        grid=(dilation, length // t),
        in_specs=[main(0), prev(1), main(1), nxt(1), prev(2), main(2), nxt(2)],
        out_specs=[
            pl.BlockSpec((None, t, GROUP_WIDTH_A), lambda r, b: (r, b, 0)),
            pl.BlockSpec((None, t, LANES), lambda r, b: (r, b, 0)),
        ],
        out_shape=[
            jax.ShapeDtypeStruct((dilation, length, GROUP_WIDTH_A), F32),
            jax.ShapeDtypeStruct((dilation, length, LANES), F32),
        ],
        scratch_shapes=[
            pltpu.VMEM((t + 2 * HALO_A, GROUP_WIDTH_A), BF16),
            pltpu.VMEM((t + 2 * HALO_A, GROUP_WIDTH_A), BF16),
        ],
        compiler_params=_params("parallel", "parallel"),
        name="dilated_group",
    )(a, a, a, a, a, a, a)


def _dilated_combine_body(*refs):
    n = N_GROUPS_A
    o_refs, l_refs = refs[:n], refs[n:2 * n]
    out_ref, o_sc, l_sc = refs[2 * n:]
    tm = out_ref.shape[0]
    for gi, (_, dilation) in enumerate(DILATED_GROUPS):
        rows = tm // dilation
        for r in range(dilation):
            dst = pl.ds(r, rows, stride=dilation) if dilation > 1 else slice(None)
            l_sc[gi, dst, :] = l_refs[gi][r]
            for hh in range(HEADS_PER_GROUP_A):
                o_sc[gi, hh, dst, :] = o_refs[gi][r, :, hh * HEAD_DIM_A:(hh + 1) * HEAD_DIM_A]
    for hh in range(HEADS_PER_GROUP_A):
        ls = [l_sc[gi, :, hh:hh + 1] for gi in range(n)]
        m = functools.reduce(jnp.maximum, ls)
        ws = [jnp.exp2(x - m) for x in ls]
        den = functools.reduce(jnp.add, ws)
        acc = functools.reduce(jnp.add, [ws[gi] * o_sc[gi, hh] for gi in range(n)])
        out_ref[:, hh * HEAD_DIM_A:(hh + 1) * HEAD_DIM_A] = (acc / den).astype(out_ref.dtype)


def dilated_attention(a_groups, *, tm=1024):
    s = a_groups[0].shape[0] * a_groups[0].shape[1]
    tm = min(tm, s)
    outs, lses = zip(*[dilated_group(a) for a in a_groups])

    def spec(dilation, width):
        return pl.BlockSpec((dilation, tm // dilation, width), lambda i: (0, i, 0))

    return pl.pallas_call(
        _dilated_combine_body,
        grid=(s // tm,),
        in_specs=[spec(d, GROUP_WIDTH_A) for _, d in DILATED_GROUPS] + [spec(d, LANES) for _, d in DILATED_GROUPS],
        out_specs=pl.BlockSpec((tm, GROUP_WIDTH_A), lambda i: (i, 0)),
        out_shape=jax.ShapeDtypeStruct((s, GROUP_WIDTH_A), BF16),
        scratch_shapes=[
            pltpu.VMEM((N_GROUPS_A, HEADS_PER_GROUP_A, tm, HEAD_DIM_A), F32),
            pltpu.VMEM((N_GROUPS_A, tm, LANES), F32),
        ],
        compiler_params=_params("parallel"),
        name="dilated_combine",
    )(*outs, *lses)


def _merge_body(oa_ref, ob_ref, h_ref, woa_ref, wob_ref, wga_ref, wgb_ref, ba_ref, bb_ref, o_ref):
    h = h_ref[...]

    def gate(wg_ref, b_ref):
        z = jnp.dot(h, wg_ref[...], preferred_element_type=F32) + b_ref[...]
        return 1.0 / (1.0 + jnp.exp(-z))

    ya = jnp.dot(oa_ref[...], woa_ref[...], preferred_element_type=F32)
    yb = jnp.dot(ob_ref[...], wob_ref[...], preferred_element_type=F32)
    o_ref[...] = (gate(wga_ref, ba_ref) * ya + gate(wgb_ref, bb_ref) * yb).astype(o_ref.dtype)


def gated_merge(oa, ob, h, w_oa, w_ob, w_g, b_g, *, tm=1024, tn=512):
    s = oa.shape[0]
    d = w_oa.shape[1]
    tm = min(tm, s)
    nj = d // tn
    return pl.pallas_call(
        _merge_body,
        grid=(s // tm, nj),
        in_specs=[
            pl.BlockSpec((tm, oa.shape[1]), lambda i, j: (i, 0)),
            pl.BlockSpec((tm, ob.shape[1]), lambda i, j: (i, 0)),
            pl.BlockSpec((tm, h.shape[1]), lambda i, j: (i, 0)),
            pl.BlockSpec((w_oa.shape[0], tn), lambda i, j: (0, j)),
            pl.BlockSpec((w_ob.shape[0], tn), lambda i, j: (0, j)),
            pl.BlockSpec((w_g.shape[0], tn), lambda i, j: (0, j)),
            pl.BlockSpec((w_g.shape[0], tn), lambda i, j: (0, j + nj)),
            pl.BlockSpec((1, tn), lambda i, j: (0, j)),
            pl.BlockSpec((1, tn), lambda i, j: (0, j + nj)),
        ],
        out_specs=pl.BlockSpec((tm, tn), lambda i, j: (i, j)),
        out_shape=jax.ShapeDtypeStruct((s, d), BF16),
        compiler_params=_params("parallel", "arbitrary"),
        name="gated_merge",
    )(oa, ob, h, w_oa, w_ob, w_g, w_g, b_g, b_g)


HALO = 16


def _ffn_body(x_ref, xp_ref, xn_ref, g_ref, gf_ref, wa_ref, wb_ref, cwa_ref, cwb_ref, cba_ref, cbb_ref, wd_ref,
              o_ref, h_sc, *, final_norm):
    i = pl.program_id(0)
    j = pl.program_id(1)
    tm = x_ref.shape[0]

    def norm(x):
        ms = jnp.mean(x * x, axis=-1, keepdims=True)
        return x * lax.rsqrt(ms + EPS) * g_ref[...]

    @pl.when(j == 0)
    def _():
        x = x_ref[...]
        o_ref[...] = x
        h_sc[HALO:HALO + tm, :] = norm(x).astype(h_sc.dtype)
        hp = jnp.where(i > 0, norm(xp_ref[...]), 0.0)
        hn = jnp.where(i < pl.num_programs(0) - 1, norm(xn_ref[...]), 0.0)
        h_sc[:HALO, :] = hp.astype(h_sc.dtype)
        h_sc[HALO + tm:, :] = hn.astype(h_sc.dtype)

    h = h_sc[...]
    rows = tm + 2 * HALO

    def conv(w_ref, cw_ref, cb_ref):
        up = jnp.dot(h, w_ref[...], preferred_element_type=F32)
        cw = cw_ref[...]
        prev = pltpu.roll(up, 1, 0)
        nxt = pltpu.roll(up, rows - 1, 0)
        u = cb_ref[...] + prev * cw[0:1] + up * cw[1:2] + nxt * cw[2:3]
        return u[HALO:HALO + tm]

    ua = conv(wa_ref, cwa_ref, cba_ref)
    ub = conv(wb_ref, cwb_ref, cbb_ref)
    act = (ua / (1.0 + jnp.exp(-ua))) * ub
    o_ref[...] += jnp.dot(act.astype(wd_ref.dtype), wd_ref[...], preferred_element_type=F32)

    if final_norm:
        @pl.when(j == pl.num_programs(1) - 1)
        def _():
            y = o_ref[...]
            ms = jnp.mean(y * y, axis=-1, keepdims=True)
            o_ref[...] = y * lax.rsqrt(ms + EPS) * gf_ref[...]


def conv_ffn(x, g, g_final, w_up, conv_w, conv_b, w_down, *, final_norm, tm=1024, tf=512):
    s, d = x.shape
    f = w_down.shape[0]
    tm = min(tm, s)
    nf = f // tf
    hb = tm // HALO
    n_halo_blocks = s // HALO
    return pl.pallas_call(
        functools.partial(_ffn_body, final_norm=final_norm),
        grid=(s // tm, nf),
        in_specs=[
            pl.BlockSpec((tm, d), lambda i, j: (i, 0)),
            pl.BlockSpec((HALO, d), lambda i, j: (jnp.maximum(i * hb - 1, 0), 0)),
            pl.BlockSpec((HALO, d), lambda i, j: (jnp.minimum((i + 1) * hb, n_halo_blocks - 1), 0)),
            pl.BlockSpec((1, d), lambda i, j: (0, 0)),
            pl.BlockSpec((1, d), lambda i, j: (0, 0)),
            pl.BlockSpec((d, tf), lambda i, j: (0, j)),
            pl.BlockSpec((d, tf), lambda i, j: (0, j + nf)),
            pl.BlockSpec((CONV_WIDTH, tf), lambda i, j: (0, j)),
            pl.BlockSpec((CONV_WIDTH, tf), lambda i, j: (0, j + nf)),
            pl.BlockSpec((1, tf), lambda i, j: (0, j)),
            pl.BlockSpec((1, tf), lambda i, j: (0, j + nf)),
            pl.BlockSpec((tf, d), lambda i, j: (j, 0)),
        ],
        out_specs=pl.BlockSpec((tm, d), lambda i, j: (i, 0)),
        out_shape=jax.ShapeDtypeStruct((s, d), F32),
        scratch_shapes=[pltpu.VMEM((tm + 2 * HALO, d), BF16)],
        compiler_params=_params("parallel", "arbitrary"),
        name="conv_ffn",
    )(x, x, x, g.reshape(1, d), g_final.reshape(1, d), w_up, w_up, conv_w, conv_w,
      conv_b.reshape(1, -1), conv_b.reshape(1, -1), w_down)


def _rope_tables(positions, dim):
    inv_freq = 1.0 / (ROPE_THETA ** (jnp.arange(0, dim, 2, dtype=F32) / dim))
    ang = positions.astype(F32)[:, None] * inv_freq
    cos = jnp.cos(ang)
    sin = jnp.sin(ang)
    return jnp.concatenate([cos, cos], axis=1), jnp.concatenate([-sin, sin], axis=1)


def kernel(x, positions, norm_mix, w_in, b_gate, norm_q, w_uq, norm_kv, w_ukv, w_oa, w_ob, w_out,
           norm_ffn, w_up, conv_w, conv_b, w_down, norm_final):
    batch, s, d = x.shape
    assert batch == 1
    depth = w_in.shape[0]
    xs = x.reshape(s, d)
    pos = positions.reshape(s)

    cos_a, sin_a = _rope_tables(pos, HEAD_DIM_A)
    scale_a = HEAD_DIM_A ** -0.5 * math.log2(math.e)
    cos_qk = jnp.stack([cos_a * scale_a, cos_a])
    sin_qk = jnp.stack([sin_a * scale_a, sin_a])
    cos_b, sin_b = _rope_tables(pos, QK_ROPE)
    pad = ((0, 0), (0, LANES - QK_ROPE))
    cos_bp, sin_bp = jnp.pad(cos_b, pad), jnp.pad(sin_b, pad)
    cos_bt = cos_b[:, QK_ROPE // 2:].T
    sin_bt = sin_b[:, QK_ROPE // 2:].T
    scale_b = QK_DIM_B ** -0.5 * math.log2(math.e)

    o_ql = 3 * WIDTH_A
    o_kvl = o_ql + Q_LORA
    o_kr = o_kvl + KV_LORA
    o_g = o_kr + QK_ROPE
    tn = 512

    for l in range(depth):
        w_in_l = w_in[l].astype(BF16)
        w_kr = jnp.pad(w_in_l[:, o_kr:o_g], ((0, 0), (0, LANES - QK_ROPE)))
        w_g = w_in_l[:, o_g:]

        tm = min(1024, s)
        w_a = w_in_l[:, :o_ql].reshape(d, N_SEC, N_GROUPS_A, GROUP_WIDTH_A).transpose(0, 2, 1, 3)
        w_cat = jnp.concatenate([w_a.reshape(d, o_ql), w_in_l[:, o_ql:o_kr]], axis=1)
        *a_groups, lat, kr, h = in_proj(xs, norm_mix[l], w_cat, w_kr, cos_qk, sin_qk, cos_bp, sin_bp, tm=tm)
        oa = dilated_attention(a_groups)

        cq_t, ckv, ckv_t = latent_norm(lat, norm_q[l], norm_kv[l])
        w_uq_t = w_uq[l].T.astype(BF16)
        w_ukv_l = w_ukv[l].reshape(KV_LORA, N_HEADS_B, QK_NOPE + V_DIM)
        w_uk = w_ukv_l[:, :, :QK_NOPE].reshape(KV_LORA, N_HEADS_B * QK_NOPE).astype(BF16)
        w_uv_t = w_ukv_l[:, :, QK_NOPE:].reshape(KV_LORA, N_HEADS_B * V_DIM).T.astype(BF16)
        heads_per_tile = 4
        tq_rows = heads_per_tile * QK_DIM_B
        qt_b = matmul(
            w_uq_t, cq_t, functools.partial(_rope_rows_epilogue, scale=scale_b), (cos_bt, sin_bt),
            [pl.BlockSpec((QK_ROPE // 2, tm), lambda i, j: (0, j))] * 2,
            jax.ShapeDtypeStruct((N_HEADS_B * QK_DIM_B, s), BF16), pl.BlockSpec((tq_rows, tm), lambda i, j: (i, j)),
            tm=tq_rows, tn=tm, name="mla_q_proj").reshape(N_HEADS_B, QK_DIM_B, s)
        vt_b = matmul(
            w_uv_t, ckv_t, _store_epilogue, (), [],
            jax.ShapeDtypeStruct((N_HEADS_B * V_DIM, s), BF16), pl.BlockSpec((tm, tm), lambda i, j: (i, j)),
            tm=tm, tn=tm, name="mla_v_proj").reshape(N_HEADS_B, V_DIM, s)
        k_nope = matmul(
            ckv, w_uk, _store_epilogue, (), [],
            jax.ShapeDtypeStruct((s, N_HEADS_B * QK_NOPE), BF16), pl.BlockSpec((tm, tn), lambda i, j: (i, j)),
            tm=tm, tn=tn, name="mla_k_proj")
        ob = dense_attention(qt_b, k_nope, kr, vt_b)

        merged = gated_merge(oa, ob, h, w_oa[l].astype(BF16), w_ob[l].astype(BF16), w_g,
                             b_gate[l].reshape(1, -1))
        xs = matmul(
            merged, w_out[l].astype(BF16), _residual_epilogue, (xs,),
            [pl.BlockSpec((tm, tn), lambda i, j: (i, j))],
            jax.ShapeDtypeStruct((s, d), F32), pl.BlockSpec((tm, tn), lambda i, j: (i, j)),
            tm=tm, tn=tn, name="proj_out")

        xs = conv_ffn(xs, norm_ffn[l], norm_final, w_up[l].astype(BF16), conv_w[l], conv_b[l],
                      w_down[l].astype(BF16), final_norm=(l == depth - 1))

    return xs.reshape(batch, s, d)
```
